```python
import math
import jax, jax.numpy as jnp
from jax import lax
import numpy as np

D_MODEL = 2048
BATCH = 2
SEQ = 16384
DEPTH = 1
DEC_BATCH = 1
DEC_SEQ = 8192
PAST_LEN = 128

HEAD_DIM = 128
MIX_WIDTH = D_MODEL
CONV_WIDTH = MIX_WIDTH // 2
N_CONV_GROUPS = CONV_WIDTH // HEAD_DIM
ATTN_WIDTH = MIX_WIDTH - CONV_WIDTH
N_Q_HEADS = ATTN_WIDTH // HEAD_DIM
N_KV_HEADS = 2
Q_PER_KV = N_Q_HEADS // N_KV_HEADS
KV_WIDTH = N_KV_HEADS * HEAD_DIM
CONV_K = 3
Q_BLOCK = 128
GRID_W = 64
AXIS_DIM = HEAD_DIM // 2
ROPE_THETA = 10000.0
N_EXPERTS = 32
TOP_K = 4
D_FF = D_MODEL
SWIGLU_LIMIT = 7.0
SWIGLU_ALPHA = 1.702
NORM_EPS = 1e-6
IN_SPLITS = (CONV_WIDTH, 2 * CONV_WIDTH, 3 * CONV_WIDTH,
             3 * CONV_WIDTH + ATTN_WIDTH,
             3 * CONV_WIDTH + ATTN_WIDTH + KV_WIDTH)
IN_WIDTH = 3 * CONV_WIDTH + ATTN_WIDTH + 2 * KV_WIDTH

kernel_name = "hybrid_conv_axialgqa_moe_encoder"


def rms_norm(x, g):
    xf = x.astype(jnp.float32)
    y = xf * lax.rsqrt(jnp.mean(xf * xf, axis=-1, keepdims=True) + NORM_EPS)
    return (y * g.astype(jnp.float32)).astype(x.dtype)


def axial_rope_tables(n_tokens):
    rows = n_tokens // GRID_W
    row = jnp.repeat(jnp.arange(rows, dtype=jnp.float32), GRID_W)
    col = jnp.tile(jnp.arange(GRID_W, dtype=jnp.float32), rows)
    inv_freq = ROPE_THETA ** (-jnp.arange(0, AXIS_DIM, 2, dtype=jnp.float32) / AXIS_DIM)
    ang_r = row[:, None] * inv_freq[None, :]
    ang_c = col[:, None] * inv_freq[None, :]
    ang = jnp.concatenate([ang_r, ang_r, ang_c, ang_c], axis=-1)
    return jnp.cos(ang), jnp.sin(ang)


def apply_axial_rope(x, cos, sin):
    xf = x.astype(jnp.float32)
    xr = xf.reshape(xf.shape[:-1] + (2, 2, AXIS_DIM // 2))
    rot = jnp.concatenate([-xr[..., 1:, :], xr[..., :1, :]], axis=-2).reshape(xf.shape)
    out = xf * cos[None, :, None, :] + rot * sin[None, :, None, :]
    return out.astype(x.dtype)


def short_conv(u, w, b):
    up = jnp.pad(u, ((0, 0), (1, 1), (0, 0)))
    return up[:, :-2] * w[0] + up[:, 1:-1] * w[1] + up[:, 2:] * w[2] + b


def block_attention(q, k, v):
    bsz, seq = q.shape[0], q.shape[1]
    n_blk = seq // Q_BLOCK
    scale = 1.0 / math.sqrt(HEAD_DIM)
    qb = q.reshape(bsz, n_blk, Q_BLOCK, N_KV_HEADS, Q_PER_KV, HEAD_DIM).transpose(1, 0, 3, 4, 2, 5)
    kt = k.transpose(0, 2, 1, 3)
    vt = v.transpose(0, 2, 1, 3)

    def one_block(q_blk):
        s = jnp.einsum('bhgqd,bhkd->bhgqk', q_blk, kt).astype(jnp.float32) * scale
        p = jax.nn.softmax(s, axis=-1)
        return jnp.einsum('bhgqk,bhkd->bhgqd', p.astype(vt.dtype), vt)

    o = lax.map(one_block, qb)
    return o.transpose(1, 0, 4, 2, 3, 5).reshape(bsz, seq, ATTN_WIDTH)


def moe_ffn(h, router_w, router_b, w_gate_up, b_gate_up, w_down, b_down):
    logits = (h @ router_w + router_b).astype(jnp.float32)
    top_v, top_i = lax.top_k(logits, TOP_K)
    top_w = jax.nn.softmax(top_v, axis=-1)
    combine = jnp.sum(jax.nn.one_hot(top_i, N_EXPERTS, dtype=jnp.float32) * top_w[..., None], axis=1)
    combine = combine.astype(h.dtype)

    def expert_step(acc, xs):
        w1, b1, w2, b2, c = xs
        gu = h @ w1 + b1
        gate, up = gu[:, :D_FF], gu[:, D_FF:]
        gate = jnp.minimum(gate, SWIGLU_LIMIT)
        up = jnp.clip(up, -SWIGLU_LIMIT, SWIGLU_LIMIT)
        glu = gate * jax.nn.sigmoid(gate * SWIGLU_ALPHA)
        y = ((up + 1.0) * glu) @ w2 + b2
        return acc + c[:, None] * y, None

    acc0 = jnp.zeros_like(h)
    out, _ = lax.scan(expert_step, acc0, (w_gate_up, b_gate_up, w_down, b_down, combine.T))
    return out


def encoder_layer(x, cos, sin, norm1_g, w_in, conv_w, conv_b, q_norm_g, k_norm_g, w_out,
                  norm2_g, router_w, router_b, w_gate_up, b_gate_up, w_down, b_down):
    bsz, seq, _ = x.shape
    h = rms_norm(x, norm1_g)
    proj = h @ w_in
    gate_b, gate_c, hv, q, k, v = jnp.split(proj, IN_SPLITS, axis=-1)
    conv_out = gate_b * short_conv(gate_c * hv, conv_w, conv_b)
    q = q.reshape(bsz, seq, N_Q_HEADS, HEAD_DIM)
    k = k.reshape(bsz, seq, N_KV_HEADS, HEAD_DIM)
    v = v.reshape(bsz, seq, N_KV_HEADS, HEAD_DIM)
    q = apply_axial_rope(rms_norm(q, q_norm_g), cos, sin)
    k = apply_axial_rope(rms_norm(k, k_norm_g), cos, sin)
    attn_out = block_attention(q, k, v)
    x = x + jnp.concatenate([conv_out, attn_out], axis=-1) @ w_out
    h2 = rms_norm(x, norm2_g).reshape(bsz * seq, D_MODEL)
    x = x + moe_ffn(h2, router_w, router_b, w_gate_up, b_gate_up, w_down, b_down).reshape(bsz, seq, D_MODEL)
    return x


def run_trunk(x, norm1_g, w_in, conv_w, conv_b, q_norm_g, k_norm_g, w_out, norm2_g,
              router_w, router_b, w_gate_up, b_gate_up, w_down, b_down, final_g):
    cos, sin = axial_rope_tables(x.shape[1])
    for l in range(DEPTH):
        x = encoder_layer(x, cos, sin, norm1_g[l], w_in[l], conv_w[l], conv_b[l], q_norm_g[l], k_norm_g[l],
                          w_out[l], norm2_g[l], router_w[l], router_b[l], w_gate_up[l], b_gate_up[l],
                          w_down[l], b_down[l])
    return rms_norm(x, final_g)


def setup_inputs(seed: int = 0) -> dict:
    key = jax.random.key(seed)
    ks = jax.random.split(key, 18)
    f32 = jnp.float32
    L = DEPTH
    nrm = lambda k, shape, s: jax.random.normal(k, shape, dtype=f32) * s
    return {
        "x_prompt": nrm(ks[0], (BATCH, SEQ, D_MODEL), 1.0),
        "x_sample": nrm(ks[1], (DEC_BATCH, DEC_SEQ, D_MODEL), 1.0),
        "norm1_g": 1.0 + nrm(ks[2], (L, D_MODEL), 0.02),
        "w_in": nrm(ks[3], (L, D_MODEL, IN_WIDTH), D_MODEL ** -0.5),
        "conv_w": nrm(ks[4], (L, CONV_K, CONV_WIDTH), CONV_K ** -0.5),
        "conv_b": nrm(ks[5], (L, CONV_WIDTH), 0.01),
        "q_norm_g": 1.0 + nrm(ks[6], (L, HEAD_DIM), 0.02),
        "k_norm_g": 1.0 + nrm(ks[7], (L, HEAD_DIM), 0.02),
        "w_out": nrm(ks[8], (L, MIX_WIDTH, D_MODEL), MIX_WIDTH ** -0.5),
        "norm2_g": 1.0 + nrm(ks[9], (L, D_MODEL), 0.02),
        "router_w": nrm(ks[10], (L, D_MODEL, N_EXPERTS), D_MODEL ** -0.5),
        "router_b": nrm(ks[11], (L, N_EXPERTS), 0.01),
        "w_gate_up": nrm(ks[12], (L, N_EXPERTS, D_MODEL, 2 * D_FF), D_MODEL ** -0.5),
        "b_gate_up": nrm(ks[13], (L, N_EXPERTS, 2 * D_FF), 0.01),
        "w_down": nrm(ks[14], (L, N_EXPERTS, D_FF, D_MODEL), D_FF ** -0.5),
        "b_down": nrm(ks[15], (L, N_EXPERTS, D_MODEL), 0.01),
        "final_g": 1.0 + nrm(ks[16], (D_MODEL,), 0.02),
    }


def reference(x_prompt, x_sample, norm1_g, w_in, conv_w, conv_b, q_norm_g, k_norm_g, w_out, norm2_g,
              router_w, router_b, w_gate_up, b_gate_up, w_down, b_down, final_g):
    y_prompt = run_trunk(x_prompt, norm1_g, w_in, conv_w, conv_b, q_norm_g, k_norm_g, w_out, norm2_g,
                         router_w, router_b, w_gate_up, b_gate_up, w_down, b_down, final_g)
    y_sample = run_trunk(x_sample, norm1_g, w_in, conv_w, conv_b, q_norm_g, k_norm_g, w_out, norm2_g,
                         router_w, router_b, w_gate_up, b_gate_up, w_down, b_down, final_g)
    return (y_prompt, y_sample)
```

```python
import functools
import math

import jax
import jax.numpy as jnp
from jax import lax
from jax.experimental import pallas as pl
from jax.experimental.pallas import tpu as pltpu

HEAD_DIM = 128
N_KV_HEADS = 2
GRID_W = 64
ROPE_THETA = 10000.0
TOP_K = 4
SWIGLU_LIMIT = 7.0
SWIGLU_ALPHA = 1.702
NORM_EPS = 1e-6

V7X_LANES = 128
V7X_SUBLANES = 8
V7X_VMEM_BYTES = 64 * 1024 * 1024
VMEM_LIMIT = V7X_VMEM_BYTES - 8 * 1024 * 1024
NEG_PAD = -1e30

F32 = jnp.float32
BF16 = jnp.bfloat16
LOG2E = math.log2(math.e)


def _pick(target, n):
    t = min(target, n)
    while n % t:
        t //= 2
    return t


def _params(n_axes, vmem=VMEM_LIMIT):
    return pltpu.CompilerParams(dimension_semantics=("arbitrary",) * n_axes, vmem_limit_bytes=vmem)


def _resident(shape):
    nd = len(shape)
    return pl.BlockSpec(shape, lambda *_: (0,) * nd, pipeline_mode=pl.Buffered(1))


def _rms(x, g):
    ms = jnp.mean(x * x, axis=-1, keepdims=True)
    return x * lax.rsqrt(ms + NORM_EPS) * g


def _in_proj_kernel(xp_ref, xs_ref, g1_ref, w_ref, qg_ref, kg_ref, cos_ref, sin_ref,
                    gb_ref, u_ref, qT_ref, k_ref, vT_ref, *, n_prompt_tiles, conv_w, n_q, q_scale):
    i = pl.program_id(0)
    x = jnp.where(i < n_prompt_tiles, xp_ref[...], xs_ref[...])
    h = _rms(x, g1_ref[...]).astype(BF16)

    def proj(c0, c1):
        return jnp.dot(h, w_ref[:, c0:c1], preferred_element_type=F32)

    gb_ref[...] = proj(0, conv_w)
    u_ref[...] = proj(conv_w, 2 * conv_w) * proj(2 * conv_w, 3 * conv_w)

    cos = cos_ref[...]
    sin = sin_ref[...]
    lane = lax.broadcasted_iota(jnp.int32, (1, HEAD_DIM), 1)
    first_half = (lane % (HEAD_DIM // 2)) < (HEAD_DIM // 4)

    def norm_rope(xh, g):
        y = _rms(xh, g)
        up = pltpu.roll(y, HEAD_DIM - HEAD_DIM // 4, 1)
        dn = pltpu.roll(y, HEAD_DIM // 4, 1)
        return y * cos + jnp.where(first_half, up, dn) * sin

    q0 = 3 * conv_w
    q = proj(q0, q0 + n_q * HEAD_DIM)
    for hd in range(n_q):
        qh = norm_rope(q[:, hd * HEAD_DIM:(hd + 1) * HEAD_DIM], qg_ref[...]) * q_scale
        qT_ref[hd] = qh.T.astype(BF16)
    k0 = q0 + n_q * HEAD_DIM
    kv = proj(k0, k0 + 2 * N_KV_HEADS * HEAD_DIM)
    for hd in range(N_KV_HEADS):
        kh = norm_rope(kv[:, hd * HEAD_DIM:(hd + 1) * HEAD_DIM], kg_ref[...])
        k_ref[:, hd * HEAD_DIM:(hd + 1) * HEAD_DIM] = kh.astype(BF16)
        vh = kv[:, (N_KV_HEADS + hd) * HEAD_DIM:(N_KV_HEADS + hd + 1) * HEAD_DIM]
        vT_ref[hd, 0] = vh.T.astype(BF16)


def _in_proj(xp, xs, g1, w_in, qg, kg, cos, sin_signed, *, s_p, s_s, kchunk):
    n_p, d = xp.shape
    n_s = xs.shape[0]
    n = n_p + n_s
    conv_w = d // 2
    n_q = (d - conv_w) // HEAD_DIM
    tm = _pick(256, math.gcd(s_p, s_s))
    npt, spt, sst = n_p // tm, s_p // tm, s_s // tm
    r = kchunk // tm
    q_scale = LOG2E / math.sqrt(HEAD_DIM)

    def pos_map(i):
        return (jnp.where(i < npt, i % spt, (i - npt) % sst), 0)

    kernel = functools.partial(_in_proj_kernel, n_prompt_tiles=npt, conv_w=conv_w, n_q=n_q, q_scale=q_scale)
    return pl.pallas_call(
        kernel,
        grid=(n // tm,),
        in_specs=[
            pl.BlockSpec((tm, d), lambda i: (jnp.minimum(i, npt - 1), 0)),
            pl.BlockSpec((tm, d), lambda i: (jnp.maximum(i - npt, 0), 0)),
            _resident((1, d)),
            _resident(w_in.shape),
            _resident((1, HEAD_DIM)),
            _resident((1, HEAD_DIM)),
            pl.BlockSpec((tm, HEAD_DIM), pos_map),
            pl.BlockSpec((tm, HEAD_DIM), pos_map),
        ],
        out_specs=[
            pl.BlockSpec((tm, conv_w), lambda i: (i, 0)),
            pl.BlockSpec((tm, conv_w), lambda i: (i, 0)),
            pl.BlockSpec((n_q, HEAD_DIM, tm), lambda i: (0, 0, i)),
            pl.BlockSpec((tm, N_KV_HEADS * HEAD_DIM), lambda i: (i, 0)),
            pl.BlockSpec((N_KV_HEADS, 1, HEAD_DIM, tm), lambda i: (0, i // r, 0, i % r)),
        ],
        out_shape=[
            jax.ShapeDtypeStruct((n, conv_w), F32),
            jax.ShapeDtypeStruct((n, conv_w), F32),
            jax.ShapeDtypeStruct((n_q, HEAD_DIM, n), BF16),
            jax.ShapeDtypeStruct((n, N_KV_HEADS * HEAD_DIM), BF16),
            jax.ShapeDtypeStruct((N_KV_HEADS, n // kchunk, HEAD_DIM, kchunk), BF16),
        ],
        compiler_params=_params(1),
        name="in_proj",
    )(xp, xs, g1, w_in, qg, kg, cos, sin_signed)


def _attention_kernel(qT_ref, k_ref, vT_ref, o_ref, m_ref, l_ref, acc_ref, *, n_chunks, kchunk, group):
    m_ref[...] = jnp.full(m_ref.shape, -jnp.inf, F32)
    l_ref[...] = jnp.zeros(l_ref.shape, F32)
    acc_ref[...] = jnp.zeros(acc_ref.shape, F32)

    def chunk(c, carry):
        r0 = pl.multiple_of(c * kchunk, kchunk)
        k_c = k_ref[pl.ds(r0, kchunk), :]
        vT_c = vT_ref[0, c]
        for g in range(group):
            sT = jnp.dot(k_c, qT_ref[g], preferred_element_type=F32)
            m_old = m_ref[g]
            m_new = jnp.maximum(m_old, jnp.max(sT, axis=0, keepdims=True))
            alpha = jnp.exp2(m_old - m_new)
            p = jnp.exp2(sT - m_new)
            l_ref[g] = alpha * l_ref[g] + jnp.sum(p, axis=0, keepdims=True)
            acc_ref[g] = alpha * acc_ref[g] + jnp.dot(vT_c, p.astype(BF16), preferred_element_type=F32)
            m_ref[g] = m_new
        return carry

    lax.fori_loop(0, n_chunks, chunk, 0)
    for g in range(group):
        o = acc_ref[g] / l_ref[g]
        o_ref[:, g * HEAD_DIM:(g + 1) * HEAD_DIM] = o.T.astype(o_ref.dtype)


def _attention(qT, k, vT, *, row_start, batch, seq, kchunk):
    n_q = qT.shape[0]
    group = n_q // N_KV_HEADS
    tq = _pick(256, seq)
    nqt = seq // tq
    q0 = row_start // tq
    s0 = row_start // seq
    n_chunks = seq // kchunk
    kernel = functools.partial(_attention_kernel, n_chunks=n_chunks, kchunk=kchunk, group=group)
    return pl.pallas_call(
        kernel,
        grid=(batch, N_KV_HEADS, nqt),
        in_specs=[
            pl.BlockSpec((group, HEAD_DIM, tq), lambda b, h, i: (h, 0, q0 + b * nqt + i)),
            pl.BlockSpec((seq, HEAD_DIM), lambda b, h, i: (s0 + b, h)),
            pl.BlockSpec((1, n_chunks, HEAD_DIM, kchunk), lambda b, h, i: (h, s0 + b, 0, 0)),
        ],
        out_specs=pl.BlockSpec((tq, group * HEAD_DIM), lambda b, h, i: (b * nqt + i, h)),
        out_shape=jax.ShapeDtypeStruct((batch * seq, n_q * HEAD_DIM), BF16),
        scratch_shapes=[
            pltpu.VMEM((group, 1, tq), F32),
            pltpu.VMEM((group, 1, tq), F32),
            pltpu.VMEM((group, HEAD_DIM, tq), F32),
        ],
        compiler_params=_params(3),
        name="attention",
    )(qT, k, vT)


def _out_proj_kernel(xp_ref, xs_ref, gb_ref, u_ref, up_ref, un_ref, ap_ref, as_ref, cw_ref, cb_ref, wo_ref,
                     g2_ref, rw_ref, rb_ref,
                     x1_ref, h2_ref, mi_ref, mw_ref, cnt_ref, ubuf,
                     *, n_prompt_tiles, spt, sst, conv_w, tm):
    i = pl.program_id(0)
    is_p = i < n_prompt_tiles
    local = jnp.where(is_p, i % spt, (i - n_prompt_tiles) % sst)
    per_seq = jnp.where(is_p, spt, sst)
    keep_prev = (local > 0).astype(F32)
    keep_next = (local < per_seq - 1).astype(F32)

    sub = V7X_SUBLANES
    ubuf[0:sub, :] = up_ref[...] * keep_prev
    ubuf[sub:sub + tm, :] = u_ref[...]
    ubuf[sub + tm:2 * sub + tm, :] = un_ref[...] * keep_next
    conv = (ubuf[sub - 1:sub - 1 + tm, :] * cw_ref[0:1, :] + ubuf[sub:sub + tm, :] * cw_ref[1:2, :]
            + ubuf[sub + 1:sub + 1 + tm, :] * cw_ref[2:3, :] + cb_ref[...])
    conv_out = (gb_ref[...] * conv).astype(BF16)
    attn = jnp.where(is_p, ap_ref[...], as_ref[...])
    x = jnp.where(is_p, xp_ref[...], xs_ref[...])
    x1 = (x + jnp.dot(conv_out, wo_ref[0:conv_w, :], preferred_element_type=F32)
          + jnp.dot(attn, wo_ref[conv_w:, :], preferred_element_type=F32))
    x1_ref[...] = x1
    h2 = _rms(x1, g2_ref[...])
    h2_ref[...] = h2

    logits = jnp.dot(h2.astype(BF16), rw_ref[...], preferred_element_type=F32) + rb_ref[...]
    lane = lax.broadcasted_iota(jnp.int32, logits.shape, 1)
    vals, hots = [], []
    work = logits
    for _ in range(TOP_K):
        m = jnp.max(work, axis=-1, keepdims=True)
        idx = jnp.min(jnp.where(work == m, lane, V7X_LANES), axis=-1, keepdims=True)
        hot = lane == idx
        vals.append(m)
        hots.append((hot, idx))
        work = jnp.where(hot, -jnp.inf, work)
    es = [jnp.exp(v - vals[0]) for v in vals]
    den = es[0] + es[1] + es[2] + es[3]

    @pl.when(i == 0)
    def _():
        cnt_ref[...] = jnp.zeros(cnt_ref.shape, F32)

    sel = jnp.zeros(logits.shape, F32)
    for hot, _ in hots:
        sel = sel + hot.astype(F32)
    row = lax.broadcasted_iota(jnp.int32, (tm, tm), 0)
    col = lax.broadcasted_iota(jnp.int32, (tm, tm), 1)
    lower = jnp.where(col < row, 1.0, 0.0).astype(BF16)
    before = jnp.dot(lower, sel.astype(BF16), preferred_element_type=F32) + cnt_ref[0:1, :]
    mi = jnp.zeros(logits.shape, jnp.int32)
    mw = jnp.zeros(logits.shape, F32)
    for kk, (hot, idx) in enumerate(hots):
        rank = jnp.sum(jnp.where(hot, before, 0.0), axis=-1, keepdims=True).astype(jnp.int32)
        mi = jnp.where(lane == kk, idx, mi)
        mi = jnp.where(lane == TOP_K + kk, rank, mi)
        mw = jnp.where(lane == kk, es[kk] / den, mw)
    mi_ref[...] = mi
    mw_ref[...] = mw
    cnt_ref[...] = cnt_ref[...] + jnp.sum(sel, axis=0, keepdims=True)


def _out_proj(xp, xs, gb, u, attn_p, attn_s, conv_w_, conv_b, w_out, g2, rw, rb, *, s_p, s_s):
    n_p, d = xp.shape
    n_s = xs.shape[0]
    n = n_p + n_s
    conv_w = gb.shape[1]
    a_w = attn_p.shape[1]
    tm = _pick(256, math.gcd(s_p, s_s))
    npt, spt, sst = n_p // tm, s_p // tm, s_s // tm
    sub = V7X_SUBLANES
    hb = tm // sub
    kernel = functools.partial(_out_proj_kernel, n_prompt_tiles=npt, spt=spt, sst=sst, conv_w=conv_w, tm=tm)
    row = lambda i: (i, 0)
    return pl.pallas_call(
        kernel,
        grid=(n // tm,),
        in_specs=[
            pl.BlockSpec((tm, d), lambda i: (jnp.minimum(i, npt - 1), 0)),
            pl.BlockSpec((tm, d), lambda i: (jnp.maximum(i - npt, 0), 0)),
            pl.BlockSpec((tm, conv_w), row),
            pl.BlockSpec((tm, conv_w), row),
            pl.BlockSpec((sub, conv_w), lambda i: (jnp.maximum(i * hb - 1, 0), 0)),
            pl.BlockSpec((sub, conv_w), lambda i: (jnp.minimum((i + 1) * hb, n // sub - 1), 0)),
            pl.BlockSpec((tm, a_w), lambda i: (jnp.minimum(i, npt - 1), 0)),
            pl.BlockSpec((tm, a_w), lambda i: (jnp.maximum(i - npt, 0), 0)),
            _resident(conv_w_.shape),
            _resident((1, conv_w)),
            _resident(w_out.shape),
            _resident((1, d)),
            _resident(rw.shape),
            _resident(rb.shape),
        ],
        out_specs=[
            pl.BlockSpec((tm, d), row),
            pl.BlockSpec((tm, d), row),
            pl.BlockSpec((tm, V7X_LANES), row),
            pl.BlockSpec((tm, V7X_LANES), row),
            pl.BlockSpec((V7X_SUBLANES, V7X_LANES), lambda i: (0, 0)),
        ],
        out_shape=[
            jax.ShapeDtypeStruct((n, d), F32),
            jax.ShapeDtypeStruct((n, d), F32),
            jax.ShapeDtypeStruct((n, V7X_LANES), jnp.int32),
            jax.ShapeDtypeStruct((n, V7X_LANES), F32),
            jax.ShapeDtypeStruct((V7X_SUBLANES, V7X_LANES), F32),
        ],
        scratch_shapes=[pltpu.VMEM((tm + 2 * sub, conv_w), F32)],
        compiler_params=_params(1),
        name="out_proj",
    )(xp, xs, gb, u, u, u, attn_p, attn_s, conv_w_, conv_b, w_out, g2, rw, rb)


def _row_copy(src, s, dst, d, sem):
    return pltpu.make_async_copy(src.at[pl.ds(s, 1)], dst.at[pl.ds(d, 1)], sem)


def _dispatch_kernel(pos_ref, h_hbm, xs_hbm, sem, *, tm, n_tiles):
    i = pl.program_id(0)

    def issue(t, carry):
        for kk in range(TOP_K):
            _row_copy(h_hbm, i * tm + t, xs_hbm, pos_ref[0, 0, t * TOP_K + kk], sem).start()
        return carry

    lax.fori_loop(0, tm, issue, 0)

    def drain(t, carry):
        for kk in range(TOP_K):
            _row_copy(h_hbm, 0, xs_hbm, 0, sem).wait()
        return carry

    @pl.when(i > 0)
    def _():
        lax.fori_loop(0, tm, drain, 0)

    @pl.when(i == n_tiles - 1)
    def _():
        lax.fori_loop(0, tm, drain, 0)


def _dispatch(pos, h2):
    n, d = h2.shape
    tm = _pick(256, n)
    n_tiles = n // tm
    pos3 = pos.reshape(n_tiles, 1, tm * TOP_K)
    kernel = functools.partial(_dispatch_kernel, tm=tm, n_tiles=n_tiles)
    return pl.pallas_call(
        kernel,
        grid=(n_tiles,),
        in_specs=[
            pl.BlockSpec((1, 1, tm * TOP_K), lambda i: (i, 0, 0), memory_space=pltpu.SMEM),
            pl.BlockSpec(memory_space=pl.ANY),
        ],
        out_specs=pl.BlockSpec(memory_space=pl.ANY),
        out_shape=jax.ShapeDtypeStruct((n * TOP_K, d), h2.dtype),
        scratch_shapes=[pltpu.SemaphoreType.DMA(())],
        compiler_params=_params(1),
        name="dispatch",
    )(pos3, h2)


def _experts_kernel(vt_ref, ve_ref, vfirst_ref, vlo_ref, vhi_ref,
                    x_ref, wg_ref, wu_ref, bg_ref, bu_ref, wd_ref, bd_ref, o_ref, xb_ref):
    v = pl.program_id(0)
    f = pl.program_id(1)
    lo = vlo_ref[v]
    hi = vhi_ref[v]

    @pl.when(hi > lo)
    def _():
        @pl.when(f == 0)
        def _():
            xb_ref[...] = x_ref[...].astype(BF16)

        xb = xb_ref[...]
        gate = jnp.dot(xb, wg_ref[0], preferred_element_type=F32) + bg_ref[0]
        up = jnp.dot(xb, wu_ref[0], preferred_element_type=F32) + bu_ref[0]
        gate = jnp.minimum(gate, SWIGLU_LIMIT)
        up = jnp.clip(up, -SWIGLU_LIMIT, SWIGLU_LIMIT)
        glu = gate * jax.nn.sigmoid(gate * SWIGLU_ALPHA)
        rows = lax.broadcasted_iota(jnp.int32, (xb.shape[0], 1), 0)
        mine = (rows >= lo) & (rows < hi)
        act = jnp.where(mine, (up + 1.0) * glu, 0.0).astype(BF16)
        y = jnp.dot(act, wd_ref[0], preferred_element_type=F32)
        bias_on = jnp.where(f == 0, 1.0, 0.0)
        y = y + jnp.where(mine, bd_ref[0] * bias_on, 0.0)
        fresh = jnp.logical_and(vfirst_ref[v] == 1, f == 0)

        @pl.when(fresh)
        def _():
            o_ref[...] = y

        @pl.when(jnp.logical_not(fresh))
        def _():
            o_ref[...] = o_ref[...] + y


def _visit_plan(counts, n_rows, tm, n_visits):
    n_exp = counts.shape[0]
    ends = jnp.cumsum(counts)
    starts = ends - counts
    first_tile = starts // tm
    last_tile = jnp.maximum(ends - 1, 0) // tm
    nvis = jnp.where(counts > 0, last_tile - first_tile + 1, 0)
    vis_end = jnp.cumsum(nvis)
    vis_start = vis_end - nvis
    total = vis_end[-1]
    v = jnp.arange(n_visits, dtype=jnp.int32)
    e = jnp.sum((v[:, None] >= vis_end[None, :]).astype(jnp.int32), axis=1)
    e = jnp.minimum(e, n_exp - 1)
    valid = v < total
    onehot = (e[:, None] == jnp.arange(n_exp, dtype=jnp.int32)[None, :]).astype(jnp.int32)
    pickv = lambda a: jnp.sum(onehot * a[None, :], axis=1)
    tile = pickv(first_tile) + (v - pickv(vis_start))
    lo = jnp.maximum(pickv(starts), tile * tm) - tile * tm
    hi = jnp.minimum(pickv(ends), (tile + 1) * tm) - tile * tm
    last_e = jnp.sum((total - 1 >= vis_end).astype(jnp.int32))
    tile = jnp.where(valid, tile, n_rows // tm - 1)
    e = jnp.where(valid, e, jnp.minimum(last_e, n_exp - 1))
    lo = jnp.where(valid, lo, 0)
    hi = jnp.where(valid, hi, 0)
    first = jnp.where(valid & (lo == 0), 1, 0)
    i32 = lambda a: a.astype(jnp.int32)
    return i32(tile), i32(e), i32(first), i32(lo), i32(hi)


def _experts(plan, xs, w_gu, b_gu, w_d, b_d):
    n_rows, d = xs.shape
    n_exp, _, two_f = w_gu.shape
    d_ff = two_f // 2
    tm = _pick(512, n_rows)
    tf = _pick(1024, d_ff)
    nf = d_ff // tf
    n_visits = plan[0].shape[0]

    def fx(f, hi, lo, v):
        return jnp.where(hi[v] > lo[v], f, 0)

    grid_spec = pltpu.PrefetchScalarGridSpec(
        num_scalar_prefetch=5,
        grid=(n_visits, nf),
        in_specs=[
            pl.BlockSpec((tm, d), lambda v, f, vt, ve, vf, lo, hi: (vt[v], 0)),
            pl.BlockSpec((1, d, tf), lambda v, f, vt, ve, vf, lo, hi: (ve[v], 0, fx(f, hi, lo, v))),
            pl.BlockSpec((1, d, tf), lambda v, f, vt, ve, vf, lo, hi: (ve[v], 0, nf + fx(f, hi, lo, v))),
            pl.BlockSpec((1, 1, tf), lambda v, f, vt, ve, vf, lo, hi: (ve[v], 0, fx(f, hi, lo, v))),
            pl.BlockSpec((1, 1, tf), lambda v, f, vt, ve, vf, lo, hi: (ve[v], 0, nf + fx(f, hi, lo, v))),
            pl.BlockSpec((1, tf, d), lambda v, f, vt, ve, vf, lo, hi: (ve[v], fx(f, hi, lo, v), 0)),
            pl.BlockSpec((1, 1, d), lambda v, f, vt, ve, vf, lo, hi: (ve[v], 0, 0)),
        ],
        out_specs=pl.BlockSpec((tm, d), lambda v, f, vt, ve, vf, lo, hi: (vt[v], 0)),
        scratch_shapes=[pltpu.VMEM((tm, d), BF16)],
    )
    return pl.pallas_call(
        _experts_kernel,
        grid_spec=grid_spec,
        out_shape=jax.ShapeDtypeStruct((n_rows, d), F32),
        compiler_params=_params(2),
        name="experts",
    )(*plan, xs, w_gu, w_gu, b_gu, b_gu, w_d, b_d)


def _combine_kernel(pos_ref, nxt_ref, y_hbm, x1_ref, w_ref, fg_ref, o_ref, buf, sem, *, tm, n_steps):
    i = pl.program_id(0)
    slot = i % 2

    def fetch(p_ref, s):
        def body(t, carry):
            for kk in range(TOP_K):
                pltpu.make_async_copy(y_hbm.at[pl.ds(p_ref[0, 0, t * TOP_K + kk], 1)],
                                      buf.at[s, kk, pl.ds(t, 1)], sem.at[s]).start()
            return carry
        lax.fori_loop(0, tm, body, 0)

    @pl.when(i == 0)
    def _():
        fetch(pos_ref, 0)

    @pl.when(i + 1 < n_steps)
    def _():
        fetch(nxt_ref, 1 - slot)

    def drain(t, carry):
        for kk in range(TOP_K):
            pltpu.make_async_copy(y_hbm.at[pl.ds(0, 1)], buf.at[slot, kk, pl.ds(t, 1)], sem.at[slot]).wait()
        return carry

    lax.fori_loop(0, tm, drain, 0)
    acc = x1_ref[...]
    for kk in range(TOP_K):
        acc = acc + w_ref[:, kk:kk + 1] * buf[slot, kk]
    o_ref[...] = _rms(acc, fg_ref[...])


def _combine(pos, y_sorted, x1, wts, fg, *, row_start, rows):
    n, d = x1.shape
    tm = _pick(128, rows)
    n_steps = rows // tm
    t0 = row_start // tm
    pos3 = pos.reshape(n // tm, 1, tm * TOP_K)
    kernel = functools.partial(_combine_kernel, tm=tm, n_steps=n_steps)
    smem = lambda im: pl.BlockSpec((1, 1, tm * TOP_K), im, memory_space=pltpu.SMEM)
    return pl.pallas_call(
        kernel,
        grid=(n_steps,),
        in_specs=[
            smem(lambda i: (t0 + i, 0, 0)),
            smem(lambda i: (t0 + jnp.minimum(i + 1, n_steps - 1), 0, 0)),
            pl.BlockSpec(memory_space=pl.ANY),
            pl.BlockSpec((tm, d), lambda i: (t0 + i, 0)),
            pl.BlockSpec((tm, V7X_LANES), lambda i: (t0 + i, 0)),
            _resident((1, d)),
        ],
        out_specs=pl.BlockSpec((tm, d), lambda i: (i, 0)),
        out_shape=jax.ShapeDtypeStruct((rows, d), F32),
        scratch_shapes=[pltpu.VMEM((2, TOP_K, tm, d), F32), pltpu.SemaphoreType.DMA((2,))],
        compiler_params=_params(1),
        name="combine",
    )(pos3, pos3, y_sorted, x1, wts, fg)


def _rope_tables(n_tokens):
    rows = n_tokens // GRID_W
    axis_dim = HEAD_DIM // 2
    row = jnp.repeat(jnp.arange(rows, dtype=F32), GRID_W)
    col = jnp.tile(jnp.arange(GRID_W, dtype=F32), rows)
    inv_freq = ROPE_THETA ** (-jnp.arange(0, axis_dim, 2, dtype=F32) / axis_dim)
    ang_r = row[:, None] * inv_freq[None, :]
    ang_c = col[:, None] * inv_freq[None, :]
    ang = jnp.concatenate([ang_r, ang_r, ang_c, ang_c], axis=-1)
    quarter = jnp.arange(HEAD_DIM) % (HEAD_DIM // 2) < HEAD_DIM // 4
    return jnp.cos(ang), jnp.where(quarter[None, :], -1.0, 1.0).astype(F32) * jnp.sin(ang)


def kernel(x_prompt, x_sample, norm1_g, w_in, conv_w, conv_b, q_norm_g, k_norm_g, w_out, norm2_g,
           router_w, router_b, w_gate_up, b_gate_up, w_down, b_down, final_g):
    assert norm1_g.shape[0] == 1, "one layer"
    b_p, s_p, d = x_prompt.shape
    b_s, s_s, _ = x_sample.shape
    n_p, n_s = b_p * s_p, b_s * s_s
    n = n_p + n_s
    n_exp = router_w.shape[-1]
    assert n_exp <= V7X_LANES and n_p % s_s == 0
    kchunk = _pick(512, math.gcd(s_p, s_s))
    xp = x_prompt.reshape(n_p, d)
    xs = x_sample.reshape(n_s, d)
    cos, sin_signed = _rope_tables(max(s_p, s_s))

    gb, u, qT, k, vT = _in_proj(
        xp, xs, norm1_g[0][None, :], w_in[0].astype(BF16), q_norm_g[0][None, :], k_norm_g[0][None, :],
        cos, sin_signed, s_p=s_p, s_s=s_s, kchunk=kchunk)
    attn_p = _attention(qT, k, vT, row_start=0, batch=b_p, seq=s_p, kchunk=kchunk)
    attn_s = _attention(qT, k, vT, row_start=n_p, batch=b_s, seq=s_s, kchunk=kchunk)

    rw = jnp.zeros((d, V7X_LANES), BF16).at[:, :n_exp].set(router_w[0].astype(BF16))
    rb = jnp.full((1, V7X_LANES), NEG_PAD, F32).at[0, :n_exp].set(router_b[0])
    x1, h2, meta_i, meta_w, cnt = _out_proj(
        xp, xs, gb, u, attn_p, attn_s, conv_w[0], conv_b[0][None, :], w_out[0].astype(BF16),
        norm2_g[0][None, :], rw, rb, s_p=s_p, s_s=s_s)

    counts = cnt[0, :n_exp].astype(jnp.int32)
    starts = jnp.cumsum(counts) - counts
    ids = meta_i[:, :TOP_K]
    ranks = meta_i[:, TOP_K:2 * TOP_K]
    onehot = ids[:, :, None] == jnp.arange(n_exp, dtype=jnp.int32)[None, None, :]
    pos = ranks + jnp.sum(jnp.where(onehot, starts[None, None, :], 0), axis=-1)

    x_sorted = _dispatch(pos, h2)
    n_rows = n * TOP_K
    tm_e = _pick(512, n_rows)
    plan = _visit_plan(counts, n_rows, tm_e, n_rows // tm_e + n_exp - 1)
    y_sorted = _experts(plan, x_sorted, w_gate_up[0].astype(BF16), b_gate_up[0][:, None, :],
                        w_down[0].astype(BF16), b_down[0][:, None, :])

    fg = final_g[None, :]
    y_p = _combine(pos, y_sorted, x1, meta_w, fg, row_start=0, rows=n_p)
    y_s = _combine(pos, y_sorted, x1, meta_w, fg, row_start=n_p, rows=n_s)
    return (y_p.reshape(b_p, s_p, d), y_s.reshape(b_s, s_s, d))
```

```python
import functools
import math

import jax
import jax.numpy as jnp
from jax import lax
from jax.experimental import pallas as pl
from jax.experimental.pallas import tpu as pltpu

HEAD_DIM = 128
N_KV_HEADS = 2
GRID_W = 64
ROPE_THETA = 10000.0
TOP_K = 4
SWIGLU_LIMIT = 7.0
SWIGLU_ALPHA = 1.702
NORM_EPS = 1e-6

V7X_LANES = 128
V7X_SUBLANES = 8
V7X_VMEM_BYTES = 64 * 1024 * 1024
VMEM_LIMIT = V7X_VMEM_BYTES - 8 * 1024 * 1024
NEG_PAD = -1e30

F32 = jnp.float32
BF16 = jnp.bfloat16
LOG2E = math.log2(math.e)


def _pick(target, n):
    t = min(target, n)
    while n % t:
        t //= 2
    return t


def _params(n_axes, vmem=VMEM_LIMIT):
    return pltpu.CompilerParams(dimension_semantics=("arbitrary",) * n_axes, vmem_limit_bytes=vmem)


def _resident(shape):
    nd = len(shape)
    return pl.BlockSpec(shape, lambda *_: (0,) * nd, pipeline_mode=pl.Buffered(1))


def _rms(x, g):
    ms = jnp.mean(x * x, axis=-1, keepdims=True)
    return x * lax.rsqrt(ms + NORM_EPS) * g


def _in_proj_kernel(xp_ref, xs_ref, g1_ref, w_ref, qg_ref, kg_ref, cos_ref, sin_ref,
                    gb_ref, u_ref, qT_ref, k_ref, vT_ref, *, n_prompt_tiles, conv_w, n_q, q_scale):
    i = pl.program_id(0)
    x = jnp.where(i < n_prompt_tiles, xp_ref[...], xs_ref[...])
    h = _rms(x, g1_ref[...]).astype(BF16)

    def proj(c0, c1):
        return jnp.dot(h, w_ref[:, c0:c1], preferred_element_type=F32)

    gb_ref[...] = proj(0, conv_w)
    u_ref[...] = proj(conv_w, 2 * conv_w) * proj(2 * conv_w, 3 * conv_w)

    cos = cos_ref[...]
    sin = sin_ref[...]
    lane = lax.broadcasted_iota(jnp.int32, (1, HEAD_DIM), 1)
    first_half = (lane % (HEAD_DIM // 2)) < (HEAD_DIM // 4)

    def norm_rope(xh, g):
        y = _rms(xh, g)
        up = pltpu.roll(y, HEAD_DIM - HEAD_DIM // 4, 1)
        dn = pltpu.roll(y, HEAD_DIM // 4, 1)
        return y * cos + jnp.where(first_half, up, dn) * sin

    q0 = 3 * conv_w
    q = proj(q0, q0 + n_q * HEAD_DIM)
    for hd in range(n_q):
        qh = norm_rope(q[:, hd * HEAD_DIM:(hd + 1) * HEAD_DIM], qg_ref[...]) * q_scale
        qT_ref[hd] = qh.T.astype(BF16)
    k0 = q0 + n_q * HEAD_DIM
    kv = proj(k0, k0 + 2 * N_KV_HEADS * HEAD_DIM)
    for hd in range(N_KV_HEADS):
        kh = norm_rope(kv[:, hd * HEAD_DIM:(hd + 1) * HEAD_DIM], kg_ref[...])
        k_ref[:, hd * HEAD_DIM:(hd + 1) * HEAD_DIM] = kh.astype(BF16)
        vh = kv[:, (N_KV_HEADS + hd) * HEAD_DIM:(N_KV_HEADS + hd + 1) * HEAD_DIM]
        vT_ref[hd, 0] = vh.T.astype(BF16)


def _in_proj(xp, xs, g1, w_in, qg, kg, cos, sin_signed, *, s_p, s_s, kchunk):
    n_p, d = xp.shape
    n_s = xs.shape[0]
    n = n_p + n_s
    conv_w = d // 2
    n_q = (d - conv_w) // HEAD_DIM
    tm = _pick(256, math.gcd(s_p, s_s))
    npt, spt, sst = n_p // tm, s_p // tm, s_s // tm
    r = kchunk // tm
    q_scale = LOG2E / math.sqrt(HEAD_DIM)

    def pos_map(i):
        return (jnp.where(i < npt, i % spt, (i - npt) % sst), 0)

    kernel = functools.partial(_in_proj_kernel, n_prompt_tiles=npt, conv_w=conv_w, n_q=n_q, q_scale=q_scale)
    return pl.pallas_call(
        kernel,
        grid=(n // tm,),
        in_specs=[
            pl.BlockSpec((tm, d), lambda i: (jnp.minimum(i, npt - 1), 0)),
            pl.BlockSpec((tm, d), lambda i: (jnp.maximum(i - npt, 0), 0)),
            _resident((1, d)),
            _resident(w_in.shape),
            _resident((1, HEAD_DIM)),
            _resident((1, HEAD_DIM)),
            pl.BlockSpec((tm, HEAD_DIM), pos_map),
            pl.BlockSpec((tm, HEAD_DIM), pos_map),
        ],
        out_specs=[
            pl.BlockSpec((tm, conv_w), lambda i: (i, 0)),
            pl.BlockSpec((tm, conv_w), lambda i: (i, 0)),
            pl.BlockSpec((n_q, HEAD_DIM, tm), lambda i: (0, 0, i)),
            pl.BlockSpec((tm, N_KV_HEADS * HEAD_DIM), lambda i: (i, 0)),
            pl.BlockSpec((N_KV_HEADS, 1, HEAD_DIM, tm), lambda i: (0, i // r, 0, i % r)),
        ],
        out_shape=[
            jax.ShapeDtypeStruct((n, conv_w), F32),
            jax.ShapeDtypeStruct((n, conv_w), F32),
            jax.ShapeDtypeStruct((n_q, HEAD_DIM, n), BF16),
            jax.ShapeDtypeStruct((n, N_KV_HEADS * HEAD_DIM), BF16),
            jax.ShapeDtypeStruct((N_KV_HEADS, n // kchunk, HEAD_DIM, kchunk), BF16),
        ],
        compiler_params=_params(1),
        name="in_proj",
    )(xp, xs, g1, w_in, qg, kg, cos, sin_signed)


def _attention_kernel(qT_ref, k_ref, vT_ref, o_ref, m_ref, l_ref, acc_ref, s_ref, mx_ref,
                      *, n_chunks, kchunk, group):
    m_ref[...] = jnp.full(m_ref.shape, -jnp.inf, F32)
    l_ref[...] = jnp.zeros(l_ref.shape, F32)
    acc_ref[...] = jnp.zeros(acc_ref.shape, F32)

    def scores(c, g, par):
        r0 = pl.multiple_of(c * kchunk, kchunk)
        sT = jnp.dot(k_ref[pl.ds(r0, kchunk), :], qT_ref[g], preferred_element_type=F32)
        s_ref[par] = sT
        mx_ref[par] = jnp.max(sT, axis=0, keepdims=True)

    def consume(c, g, par):
        m_old = m_ref[g]
        m_new = jnp.maximum(m_old, mx_ref[par])
        alpha = jnp.exp2(m_old - m_new)
        p = jnp.exp2(s_ref[par] - m_new)
        l_ref[g] = alpha * l_ref[g] + jnp.sum(p, axis=0, keepdims=True)
        acc_ref[g] = alpha * acc_ref[g] + jnp.dot(vT_ref[0, c], p.astype(BF16), preferred_element_type=F32)
        m_ref[g] = m_new

    def parity(c, g):
        return (g % 2) if group % 2 == 0 else (c * group + g) % 2

    scores(0, 0, 0)

    def chunk(c, carry):
        for g in range(group):
            if g + 1 < group:
                scores(c, g + 1, parity(c, g + 1))
            else:
                scores(jnp.minimum(c + 1, n_chunks - 1), 0, parity(c + 1, 0))
            consume(c, g, parity(c, g))
        return carry

    lax.fori_loop(0, n_chunks, chunk, 0)
    for g in range(group):
        o = acc_ref[g] / l_ref[g]
        o_ref[:, g * HEAD_DIM:(g + 1) * HEAD_DIM] = o.T.astype(o_ref.dtype)


def _attention(qT, k, vT, *, row_start, batch, seq, kchunk):
    n_q = qT.shape[0]
    group = n_q // N_KV_HEADS
    tq = _pick(256, seq)
    nqt = seq // tq
    q0 = row_start // tq
    s0 = row_start // seq
    n_chunks = seq // kchunk
    kernel = functools.partial(_attention_kernel, n_chunks=n_chunks, kchunk=kchunk, group=group)
    return pl.pallas_call(
        kernel,
        grid=(batch, N_KV_HEADS, nqt),
        in_specs=[
            pl.BlockSpec((group, HEAD_DIM, tq), lambda b, h, i: (h, 0, q0 + b * nqt + i)),
            pl.BlockSpec((seq, HEAD_DIM), lambda b, h, i: (s0 + b, h)),
            pl.BlockSpec((1, n_chunks, HEAD_DIM, kchunk), lambda b, h, i: (h, s0 + b, 0, 0)),
        ],
        out_specs=pl.BlockSpec((tq, group * HEAD_DIM), lambda b, h, i: (b * nqt + i, h)),
        out_shape=jax.ShapeDtypeStruct((batch * seq, n_q * HEAD_DIM), BF16),
        scratch_shapes=[
            pltpu.VMEM((group, 1, tq), F32),
            pltpu.VMEM((group, 1, tq), F32),
            pltpu.VMEM((group, HEAD_DIM, tq), F32),
            pltpu.VMEM((2, kchunk, tq), F32),
            pltpu.VMEM((2, 1, tq), F32),
        ],
        compiler_params=_params(3),
        name="attention",
    )(qT, k, vT)


def _out_proj_kernel(xp_ref, xs_ref, gb_ref, u_ref, up_ref, un_ref, ap_ref, as_ref, cw_ref, cb_ref, wo_ref,
                     g2_ref, rw_ref, rb_ref,
                     x1_ref, h2_ref, mi_ref, mw_ref, cnt_ref, ubuf,
                     *, n_prompt_tiles, spt, sst, conv_w, tm):
    i = pl.program_id(0)
    is_p = i < n_prompt_tiles
    local = jnp.where(is_p, i % spt, (i - n_prompt_tiles) % sst)
    per_seq = jnp.where(is_p, spt, sst)
    keep_prev = (local > 0).astype(F32)
    keep_next = (local < per_seq - 1).astype(F32)

    sub = V7X_SUBLANES
    ubuf[0:sub, :] = up_ref[...] * keep_prev
    ubuf[sub:sub + tm, :] = u_ref[...]
    ubuf[sub + tm:2 * sub + tm, :] = un_ref[...] * keep_next
    conv = (ubuf[sub - 1:sub - 1 + tm, :] * cw_ref[0:1, :] + ubuf[sub:sub + tm, :] * cw_ref[1:2, :]
            + ubuf[sub + 1:sub + 1 + tm, :] * cw_ref[2:3, :] + cb_ref[...])
    conv_out = (gb_ref[...] * conv).astype(BF16)
    attn = jnp.where(is_p, ap_ref[...], as_ref[...])
    x = jnp.where(is_p, xp_ref[...], xs_ref[...])
    x1 = (x + jnp.dot(conv_out, wo_ref[0:conv_w, :], preferred_element_type=F32)
          + jnp.dot(attn, wo_ref[conv_w:, :], preferred_element_type=F32))
    x1_ref[...] = x1
    h2 = _rms(x1, g2_ref[...])
    h2_ref[...] = h2

    logits = jnp.dot(h2.astype(BF16), rw_ref[...], preferred_element_type=F32) + rb_ref[...]
    lane = lax.broadcasted_iota(jnp.int32, logits.shape, 1)
    vals, hots = [], []
    work = logits
    for _ in range(TOP_K):
        m = jnp.max(work, axis=-1, keepdims=True)
        idx = jnp.min(jnp.where(work == m, lane, V7X_LANES), axis=-1, keepdims=True)
        hot = lane == idx
        vals.append(m)
        hots.append((hot, idx))
        work = jnp.where(hot, -jnp.inf, work)
    es = [jnp.exp(v - vals[0]) for v in vals]
    den = es[0] + es[1] + es[2] + es[3]

    @pl.when(i == 0)
    def _():
        cnt_ref[...] = jnp.zeros(cnt_ref.shape, F32)

    sel = jnp.zeros(logits.shape, F32)
    for hot, _ in hots:
        sel = sel + hot.astype(F32)
    row = lax.broadcasted_iota(jnp.int32, (tm, tm), 0)
    col = lax.broadcasted_iota(jnp.int32, (tm, tm), 1)
    lower = jnp.where(col < row, 1.0, 0.0).astype(BF16)
    before = jnp.dot(lower, sel.astype(BF16), preferred_element_type=F32) + cnt_ref[0:1, :]
    mi = jnp.zeros(logits.shape, jnp.int32)
    mw = jnp.zeros(logits.shape, F32)
    for kk, (hot, idx) in enumerate(hots):
        rank = jnp.sum(jnp.where(hot, before, 0.0), axis=-1, keepdims=True).astype(jnp.int32)
        mi = jnp.where(lane == kk, idx, mi)
        mi = jnp.where(lane == TOP_K + kk, rank, mi)
        mw = jnp.where(lane == kk, es[kk] / den, mw)
    mi_ref[...] = mi
    mw_ref[...] = mw
    cnt_ref[...] = cnt_ref[...] + jnp.sum(sel, axis=0, keepdims=True)


def _out_proj(xp, xs, gb, u, attn_p, attn_s, conv_w_, conv_b, w_out, g2, rw, rb, *, s_p, s_s):
    n_p, d = xp.shape
    n_s = xs.shape[0]
    n = n_p + n_s
    conv_w = gb.shape[1]
    a_w = attn_p.shape[1]
    tm = _pick(256, math.gcd(s_p, s_s))
    npt, spt, sst = n_p // tm, s_p // tm, s_s // tm
    sub = V7X_SUBLANES
    hb = tm // sub
    kernel = functools.partial(_out_proj_kernel, n_prompt_tiles=npt, spt=spt, sst=sst, conv_w=conv_w, tm=tm)
    row = lambda i: (i, 0)
    return pl.pallas_call(
        kernel,
        grid=(n // tm,),
        in_specs=[
            pl.BlockSpec((tm, d), lambda i: (jnp.minimum(i, npt - 1), 0)),
            pl.BlockSpec((tm, d), lambda i: (jnp.maximum(i - npt, 0), 0)),
            pl.BlockSpec((tm, conv_w), row),
            pl.BlockSpec((tm, conv_w), row),
            pl.BlockSpec((sub, conv_w), lambda i: (jnp.maximum(i * hb - 1, 0), 0)),
            pl.BlockSpec((sub, conv_w), lambda i: (jnp.minimum((i + 1) * hb, n // sub - 1), 0)),
            pl.BlockSpec((tm, a_w), lambda i: (jnp.minimum(i, npt - 1), 0)),
            pl.BlockSpec((tm, a_w), lambda i: (jnp.maximum(i - npt, 0), 0)),
            _resident(conv_w_.shape),
            _resident((1, conv_w)),
            _resident(w_out.shape),
            _resident((1, d)),
            _resident(rw.shape),
            _resident(rb.shape),
        ],
        out_specs=[
            pl.BlockSpec((tm, d), row),
            pl.BlockSpec((tm, d), row),
            pl.BlockSpec((tm, V7X_LANES), row),
            pl.BlockSpec((tm, V7X_LANES), row),
            pl.BlockSpec((V7X_SUBLANES, V7X_LANES), lambda i: (0, 0)),
        ],
        out_shape=[
            jax.ShapeDtypeStruct((n, d), F32),
            jax.ShapeDtypeStruct((n, d), F32),
            jax.ShapeDtypeStruct((n, V7X_LANES), jnp.int32),
            jax.ShapeDtypeStruct((n, V7X_LANES), F32),
            jax.ShapeDtypeStruct((V7X_SUBLANES, V7X_LANES), F32),
        ],
        scratch_shapes=[pltpu.VMEM((tm + 2 * sub, conv_w), F32)],
        compiler_params=_params(1),
        name="out_proj",
    )(xp, xs, gb, u, u, u, attn_p, attn_s, conv_w_, conv_b, w_out, g2, rw, rb)


def _dispatch_kernel(pos_ref, h_hbm, xs_hbm, buf, sem_in, sem_out, *, tm, n_tiles):
    i = pl.program_id(0)
    slot = i % 2

    def load(tile, s):
        return pltpu.make_async_copy(h_hbm.at[pl.ds(tile * tm, tm)], buf.at[s], sem_in.at[s])

    def row_out(s, t, dst_row):
        return pltpu.make_async_copy(buf.at[s, pl.ds(t, 1)], xs_hbm.at[pl.ds(dst_row, 1)], sem_out.at[s])

    def drain(s):
        def body(t, carry):
            for _ in range(TOP_K):
                row_out(s, 0, 0).wait()
            return carry
        lax.fori_loop(0, tm, body, 0)

    @pl.when(i == 0)
    def _():
        load(0, 0).start()

    @pl.when(i > 0)
    def _():
        drain(1 - slot)

    @pl.when(i + 1 < n_tiles)
    def _():
        load(i + 1, 1 - slot).start()

    load(i, slot).wait()

    def issue(t, carry):
        for kk in range(TOP_K):
            row_out(slot, t, pos_ref[0, 0, t * TOP_K + kk]).start()
        return carry

    lax.fori_loop(0, tm, issue, 0)

    @pl.when(i == n_tiles - 1)
    def _():
        drain(slot)


def _dispatch(pos, h2):
    n, d = h2.shape
    tm = _pick(256, n)
    n_tiles = n // tm
    pos3 = pos.reshape(n_tiles, 1, tm * TOP_K)
    kernel = functools.partial(_dispatch_kernel, tm=tm, n_tiles=n_tiles)
    return pl.pallas_call(
        kernel,
        grid=(n_tiles,),
        in_specs=[
            pl.BlockSpec((1, 1, tm * TOP_K), lambda i: (i, 0, 0), memory_space=pltpu.SMEM),
            pl.BlockSpec(memory_space=pl.ANY),
        ],
        out_specs=pl.BlockSpec(memory_space=pl.ANY),
        out_shape=jax.ShapeDtypeStruct((n * TOP_K, d), h2.dtype),
        scratch_shapes=[pltpu.VMEM((2, tm, d), h2.dtype), pltpu.SemaphoreType.DMA((2,)),
                        pltpu.SemaphoreType.DMA((2,))],
        compiler_params=_params(1),
        name="dispatch",
    )(pos3, h2)


def _experts_kernel(vt_ref, ve_ref, vfirst_ref, vlo_ref, vhi_ref,
                    x_ref, wg_ref, wu_ref, bg_ref, bu_ref, wd_ref, bd_ref, o_ref, xb_ref):
    v = pl.program_id(0)
    f = pl.program_id(1)
    lo = vlo_ref[v]
    hi = vhi_ref[v]

    @pl.when(hi > lo)
    def _():
        @pl.when(f == 0)
        def _():
            xb_ref[...] = x_ref[...].astype(BF16)

        xb = xb_ref[...]
        gate = jnp.dot(xb, wg_ref[0], preferred_element_type=F32) + bg_ref[0]
        up = jnp.dot(xb, wu_ref[0], preferred_element_type=F32) + bu_ref[0]
        gate = jnp.minimum(gate, SWIGLU_LIMIT)
        up = jnp.clip(up, -SWIGLU_LIMIT, SWIGLU_LIMIT)
        glu = gate * jax.nn.sigmoid(gate * SWIGLU_ALPHA)
        rows = lax.broadcasted_iota(jnp.int32, (xb.shape[0], 1), 0)
        mine = (rows >= lo) & (rows < hi)
        act = jnp.where(mine, (up + 1.0) * glu, 0.0).astype(BF16)
        y = jnp.dot(act, wd_ref[0], preferred_element_type=F32)
        bias_on = jnp.where(f == 0, 1.0, 0.0)
        y = y + jnp.where(mine, bd_ref[0] * bias_on, 0.0)
        fresh = jnp.logical_and(vfirst_ref[v] == 1, f == 0)

        @pl.when(fresh)
        def _():
            o_ref[...] = y

        @pl.when(jnp.logical_not(fresh))
        def _():
            o_ref[...] = o_ref[...] + y


def _visit_plan(counts, n_rows, tm, n_visits):
    n_exp = counts.shape[0]
    ends = jnp.cumsum(counts)
    starts = ends - counts
    first_tile = starts // tm
    last_tile = jnp.maximum(ends - 1, 0) // tm
    nvis = jnp.where(counts > 0, last_tile - first_tile + 1, 0)
    vis_end = jnp.cumsum(nvis)
    vis_start = vis_end - nvis
    total = vis_end[-1]
    v = jnp.arange(n_visits, dtype=jnp.int32)
    e = jnp.sum((v[:, None] >= vis_end[None, :]).astype(jnp.int32), axis=1)
    e = jnp.minimum(e, n_exp - 1)
    valid = v < total
    onehot = (e[:, None] == jnp.arange(n_exp, dtype=jnp.int32)[None, :]).astype(jnp.int32)
    pickv = lambda a: jnp.sum(onehot * a[None, :], axis=1)
    tile = pickv(first_tile) + (v - pickv(vis_start))
    lo = jnp.maximum(pickv(starts), tile * tm) - tile * tm
    hi = jnp.minimum(pickv(ends), (tile + 1) * tm) - tile * tm
    last_e = jnp.sum((total - 1 >= vis_end).astype(jnp.int32))
    tile = jnp.where(valid, tile, n_rows // tm - 1)
    e = jnp.where(valid, e, jnp.minimum(last_e, n_exp - 1))
    lo = jnp.where(valid, lo, 0)
    hi = jnp.where(valid, hi, 0)
    first = jnp.where(valid & (lo == 0), 1, 0)
    i32 = lambda a: a.astype(jnp.int32)
    return i32(tile), i32(e), i32(first), i32(lo), i32(hi)


def _experts(plan, xs, w_gu, b_gu, w_d, b_d):
    n_rows, d = xs.shape
    n_exp, _, two_f = w_gu.shape
    d_ff = two_f // 2
    tm = _pick(512, n_rows)
    tf = _pick(1024, d_ff)
    nf = d_ff // tf
    n_visits = plan[0].shape[0]

    def fx(f, hi, lo, v):
        return jnp.where(hi[v] > lo[v], f, 0)

    grid_spec = pltpu.PrefetchScalarGridSpec(
        num_scalar_prefetch=5,
        grid=(n_visits, nf),
        in_specs=[
            pl.BlockSpec((tm, d), lambda v, f, vt, ve, vf, lo, hi: (vt[v], 0)),
            pl.BlockSpec((1, d, tf), lambda v, f, vt, ve, vf, lo, hi: (ve[v], 0, fx(f, hi, lo, v))),
            pl.BlockSpec((1, d, tf), lambda v, f, vt, ve, vf, lo, hi: (ve[v], 0, nf + fx(f, hi, lo, v))),
            pl.BlockSpec((1, 1, tf), lambda v, f, vt, ve, vf, lo, hi: (ve[v], 0, fx(f, hi, lo, v))),
            pl.BlockSpec((1, 1, tf), lambda v, f, vt, ve, vf, lo, hi: (ve[v], 0, nf + fx(f, hi, lo, v))),
            pl.BlockSpec((1, tf, d), lambda v, f, vt, ve, vf, lo, hi: (ve[v], fx(f, hi, lo, v), 0)),
            pl.BlockSpec((1, 1, d), lambda v, f, vt, ve, vf, lo, hi: (ve[v], 0, 0)),
        ],
        out_specs=pl.BlockSpec((tm, d), lambda v, f, vt, ve, vf, lo, hi: (vt[v], 0)),
        scratch_shapes=[pltpu.VMEM((tm, d), BF16)],
    )
    return pl.pallas_call(
        _experts_kernel,
        grid_spec=grid_spec,
        out_shape=jax.ShapeDtypeStruct((n_rows, d), F32),
        compiler_params=_params(2),
        name="experts",
    )(*plan, xs, w_gu, w_gu, b_gu, b_gu, w_d, b_d)


def _combine_kernel(pos_ref, nxt_ref, y_hbm, x1_ref, w_ref, fg_ref, o_ref, buf, sem, *, tm, n_steps):
    i = pl.program_id(0)
    slot = i % 2

    def fetch(p_ref, s):
        def body(t, carry):
            for kk in range(TOP_K):
                pltpu.make_async_copy(y_hbm.at[pl.ds(p_ref[0, 0, t * TOP_K + kk], 1)],
                                      buf.at[s, kk, pl.ds(t, 1)], sem.at[s]).start()
            return carry
        lax.fori_loop(0, tm, body, 0)

    @pl.when(i == 0)
    def _():
        fetch(pos_ref, 0)

    @pl.when(i + 1 < n_steps)
    def _():
        fetch(nxt_ref, 1 - slot)

    def drain(t, carry):
        for kk in range(TOP_K):
            pltpu.make_async_copy(y_hbm.at[pl.ds(0, 1)], buf.at[slot, kk, pl.ds(t, 1)], sem.at[slot]).wait()
        return carry

    lax.fori_loop(0, tm, drain, 0)
    acc = x1_ref[...]
    for kk in range(TOP_K):
        acc = acc + w_ref[:, kk:kk + 1] * buf[slot, kk]
    o_ref[...] = _rms(acc, fg_ref[...])


def _combine(pos, y_sorted, x1, wts, fg, *, row_start, rows):
    n, d = x1.shape
    tm = _pick(128, rows)
    n_steps = rows // tm
    t0 = row_start // tm
    pos3 = pos.reshape(n // tm, 1, tm * TOP_K)
    kernel = functools.partial(_combine_kernel, tm=tm, n_steps=n_steps)
    smem = lambda im: pl.BlockSpec((1, 1, tm * TOP_K), im, memory_space=pltpu.SMEM)
    return pl.pallas_call(
        kernel,
        grid=(n_steps,),
        in_specs=[
            smem(lambda i: (t0 + i, 0, 0)),
            smem(lambda i: (t0 + jnp.minimum(i + 1, n_steps - 1), 0, 0)),
            pl.BlockSpec(memory_space=pl.ANY),
            pl.BlockSpec((tm, d), lambda i: (t0 + i, 0)),
            pl.BlockSpec((tm, V7X_LANES), lambda i: (t0 + i, 0)),
            _resident((1, d)),
        ],
        out_specs=pl.BlockSpec((tm, d), lambda i: (i, 0)),
        out_shape=jax.ShapeDtypeStruct((rows, d), F32),
        scratch_shapes=[pltpu.VMEM((2, TOP_K, tm, d), F32), pltpu.SemaphoreType.DMA((2,))],
        compiler_params=_params(1),
        name="combine",
    )(pos3, pos3, y_sorted, x1, wts, fg)


def _rope_tables(n_tokens):
    rows = n_tokens // GRID_W
    axis_dim = HEAD_DIM // 2
    row = jnp.repeat(jnp.arange(rows, dtype=F32), GRID_W)
    col = jnp.tile(jnp.arange(GRID_W, dtype=F32), rows)
    inv_freq = ROPE_THETA ** (-jnp.arange(0, axis_dim, 2, dtype=F32) / axis_dim)
    ang_r = row[:, None] * inv_freq[None, :]
    ang_c = col[:, None] * inv_freq[None, :]
    ang = jnp.concatenate([ang_r, ang_r, ang_c, ang_c], axis=-1)
    quarter = jnp.arange(HEAD_DIM) % (HEAD_DIM // 2) < HEAD_DIM // 4
    return jnp.cos(ang), jnp.where(quarter[None, :], -1.0, 1.0).astype(F32) * jnp.sin(ang)


def kernel(x_prompt, x_sample, norm1_g, w_in, conv_w, conv_b, q_norm_g, k_norm_g, w_out, norm2_g,
           router_w, router_b, w_gate_up, b_gate_up, w_down, b_down, final_g):
    assert norm1_g.shape[0] == 1, "one layer"
    b_p, s_p, d = x_prompt.shape
    b_s, s_s, _ = x_sample.shape
    n_p, n_s = b_p * s_p, b_s * s_s
    n = n_p + n_s
    n_exp = router_w.shape[-1]
    assert n_exp <= V7X_LANES and n_p % s_s == 0
    kchunk = _pick(1024, math.gcd(s_p, s_s))
    xp = x_prompt.reshape(n_p, d)
    xs = x_sample.reshape(n_s, d)
    cos, sin_signed = _rope_tables(max(s_p, s_s))

    gb, u, qT, k, vT = _in_proj(
        xp, xs, norm1_g[0][None, :], w_in[0].astype(BF16), q_norm_g[0][None, :], k_norm_g[0][None, :],
        cos, sin_signed, s_p=s_p, s_s=s_s, kchunk=kchunk)
    attn_p = _attention(qT, k, vT, row_start=0, batch=b_p, seq=s_p, kchunk=kchunk)
    attn_s = _attention(qT, k, vT, row_start=n_p, batch=b_s, seq=s_s, kchunk=kchunk)

    rw = jnp.zeros((d, V7X_LANES), BF16).at[:, :n_exp].set(router_w[0].astype(BF16))
    rb = jnp.full((1, V7X_LANES), NEG_PAD, F32).at[0, :n_exp].set(router_b[0])
    x1, h2, meta_i, meta_w, cnt = _out_proj(
        xp, xs, gb, u, attn_p, attn_s, conv_w[0], conv_b[0][None, :], w_out[0].astype(BF16),
        norm2_g[0][None, :], rw, rb, s_p=s_p, s_s=s_s)

    counts = cnt[0, :n_exp].astype(jnp.int32)
    starts = jnp.cumsum(counts) - counts
    ids = meta_i[:, :TOP_K]
    ranks = meta_i[:, TOP_K:2 * TOP_K]
    onehot = ids[:, :, None] == jnp.arange(n_exp, dtype=jnp.int32)[None, None, :]
    pos = ranks + jnp.sum(jnp.where(onehot, starts[None, None, :], 0), axis=-1)

    x_sorted = _dispatch(pos, h2)
    n_rows = n * TOP_K
    tm_e = _pick(512, n_rows)
    plan = _visit_plan(counts, n_rows, tm_e, n_rows // tm_e + n_exp - 1)
    y_sorted = _experts(plan, x_sorted, w_gate_up[0].astype(BF16), b_gate_up[0][:, None, :],
                        w_down[0].astype(BF16), b_down[0][:, None, :])

    fg = final_g[None, :]
    y_p = _combine(pos, y_sorted, x1, meta_w, fg, row_start=0, rows=n_p)
    y_s = _combine(pos, y_sorted, x1, meta_w, fg, row_start=n_p, rows=n_s)
    return (y_p.reshape(b_p, s_p, d), y_s.reshape(b_s, s_s, d))
```

```python
import functools
import math

import jax
import jax.numpy as jnp
from jax import lax
from jax.experimental import pallas as pl
from jax.experimental.pallas import tpu as pltpu

HEAD_DIM = 128
N_KV_HEADS = 2
GRID_W = 64
ROPE_THETA = 10000.0
TOP_K = 4
SWIGLU_LIMIT = 7.0
SWIGLU_ALPHA = 1.702
NORM_EPS = 1e-6

V7X_LANES = 128
V7X_SUBLANES = 8
V7X_VMEM_BYTES = 64 * 1024 * 1024
VMEM_LIMIT = V7X_VMEM_BYTES - 8 * 1024 * 1024
NEG_PAD = -1e30

F32 = jnp.float32
BF16 = jnp.bfloat16
LOG2E = math.log2(math.e)


def _pick(target, n):
    t = min(target, n)
    while n % t:
        t //= 2
    return t


def _params(n_axes, vmem=VMEM_LIMIT):
    return pltpu.CompilerParams(dimension_semantics=("arbitrary",) * n_axes, vmem_limit_bytes=vmem)


def _resident(shape):
    nd = len(shape)
    return pl.BlockSpec(shape, lambda *_: (0,) * nd, pipeline_mode=pl.Buffered(1))


def _rms(x, g):
    ms = jnp.mean(x * x, axis=-1, keepdims=True)
    return x * lax.rsqrt(ms + NORM_EPS) * g


def _in_proj_kernel(xp_ref, xs_ref, g1_ref, w_ref, qg_ref, kg_ref, cos_ref, sin_ref,
                    gb_ref, u_ref, qT_ref, k_ref, vT_ref, *, n_prompt_tiles, conv_w, n_q, q_scale):
    i = pl.program_id(0)
    x = jnp.where(i < n_prompt_tiles, xp_ref[...], xs_ref[...])
    h = _rms(x, g1_ref[...]).astype(BF16)

    def proj(c0, c1):
        return jnp.dot(h, w_ref[:, c0:c1], preferred_element_type=F32)

    gb_ref[...] = proj(0, conv_w)
    u_ref[...] = proj(conv_w, 2 * conv_w) * proj(2 * conv_w, 3 * conv_w)

    cos = cos_ref[...]
    sin = sin_ref[...]
    lane = lax.broadcasted_iota(jnp.int32, (1, HEAD_DIM), 1)
    first_half = (lane % (HEAD_DIM // 2)) < (HEAD_DIM // 4)

    def norm_rope(xh, g):
        y = _rms(xh, g)
        up = pltpu.roll(y, HEAD_DIM - HEAD_DIM // 4, 1)
        dn = pltpu.roll(y, HEAD_DIM // 4, 1)
        return y * cos + jnp.where(first_half, up, dn) * sin

    q0 = 3 * conv_w
    q = proj(q0, q0 + n_q * HEAD_DIM)
    for hd in range(n_q):
        qh = norm_rope(q[:, hd * HEAD_DIM:(hd + 1) * HEAD_DIM], qg_ref[...]) * q_scale
        qT_ref[hd] = qh.T.astype(BF16)
    k0 = q0 + n_q * HEAD_DIM
    kv = proj(k0, k0 + 2 * N_KV_HEADS * HEAD_DIM)
    for hd in range(N_KV_HEADS):
        kh = norm_rope(kv[:, hd * HEAD_DIM:(hd + 1) * HEAD_DIM], kg_ref[...])
        k_ref[:, hd * HEAD_DIM:(hd + 1) * HEAD_DIM] = kh.astype(BF16)
        vh = kv[:, (N_KV_HEADS + hd) * HEAD_DIM:(N_KV_HEADS + hd + 1) * HEAD_DIM]
        vT_ref[hd, 0] = vh.T.astype(BF16)


def _in_proj(xp, xs, g1, w_in, qg, kg, cos, sin_signed, *, s_p, s_s, kchunk):
    n_p, d = xp.shape
    n_s = xs.shape[0]
    n = n_p + n_s
    conv_w = d // 2
    n_q = (d - conv_w) // HEAD_DIM
    tm = _pick(256, math.gcd(s_p, s_s))
    npt, spt, sst = n_p // tm, s_p // tm, s_s // tm
    r = kchunk // tm
    q_scale = LOG2E / math.sqrt(HEAD_DIM)

    def pos_map(i):
        return (jnp.where(i < npt, i % spt, (i - npt) % sst), 0)

    kernel = functools.partial(_in_proj_kernel, n_prompt_tiles=npt, conv_w=conv_w, n_q=n_q, q_scale=q_scale)
    return pl.pallas_call(
        kernel,
        grid=(n // tm,),
        in_specs=[
            pl.BlockSpec((tm, d), lambda i: (jnp.minimum(i, npt - 1), 0)),
            pl.BlockSpec((tm, d), lambda i: (jnp.maximum(i - npt, 0), 0)),
            _resident((1, d)),
            _resident(w_in.shape),
            _resident((1, HEAD_DIM)),
            _resident((1, HEAD_DIM)),
            pl.BlockSpec((tm, HEAD_DIM), pos_map),
            pl.BlockSpec((tm, HEAD_DIM), pos_map),
        ],
        out_specs=[
            pl.BlockSpec((tm, conv_w), lambda i: (i, 0)),
            pl.BlockSpec((tm, conv_w), lambda i: (i, 0)),
            pl.BlockSpec((n_q, HEAD_DIM, tm), lambda i: (0, 0, i)),
            pl.BlockSpec((tm, N_KV_HEADS * HEAD_DIM), lambda i: (i, 0)),
            pl.BlockSpec((N_KV_HEADS, 1, HEAD_DIM, tm), lambda i: (0, i // r, 0, i % r)),
        ],
        out_shape=[
            jax.ShapeDtypeStruct((n, conv_w), F32),
            jax.ShapeDtypeStruct((n, conv_w), F32),
            jax.ShapeDtypeStruct((n_q, HEAD_DIM, n), BF16),
            jax.ShapeDtypeStruct((n, N_KV_HEADS * HEAD_DIM), BF16),
            jax.ShapeDtypeStruct((N_KV_HEADS, n // kchunk, HEAD_DIM, kchunk), BF16),
        ],
        compiler_params=_params(1),
        name="in_proj",
    )(xp, xs, g1, w_in, qg, kg, cos, sin_signed)


def _attention_kernel(qT_ref, k_ref, vT_ref, o_ref, m_ref, l_ref, acc_ref, s_ref, mx_ref,
                      *, n_chunks, kchunk, ksub, group):
    m_ref[...] = jnp.full(m_ref.shape, -jnp.inf, F32)
    l_ref[...] = jnp.zeros(l_ref.shape, F32)
    acc_ref[...] = jnp.zeros(acc_ref.shape, F32)
    sub = V7X_SUBLANES

    n_sub = kchunk // ksub

    def scores_sub(c, g, par, j, mx):
        r0 = pl.multiple_of(c * kchunk + j * ksub, ksub)
        sT = jnp.dot(k_ref[pl.ds(r0, ksub), :], qT_ref[g], preferred_element_type=F32)
        s_ref[par, j * ksub:(j + 1) * ksub, :] = sT
        cm = jnp.max(sT.reshape(ksub // sub, sub, sT.shape[1]), axis=0)
        return cm if mx is None else jnp.maximum(mx, cm)

    def consume_sub(c, par, j, m_new, pv, l):
        p = jnp.exp2(s_ref[par, j * ksub:(j + 1) * ksub, :] - m_new)
        d = jnp.dot(vT_ref[0, c, :, j * ksub:(j + 1) * ksub], p.astype(BF16), preferred_element_type=F32)
        ps = jnp.sum(p.reshape(ksub // sub, sub, p.shape[1]), axis=0)
        return (d if pv is None else pv + d), (ps if l is None else l + ps)

    def pair(c_next, g_next, par_next, c, g, par):
        m_old = m_ref[g]
        m_new = jnp.maximum(m_old, jnp.max(mx_ref[par], axis=0, keepdims=True))
        alpha = jnp.exp2(m_old - m_new)
        mx = pv = l = None
        for j in range(n_sub):
            mx = scores_sub(c_next, g_next, par_next, j, mx)
            pv, l = consume_sub(c, par, j, m_new, pv, l)
        mx_ref[par_next] = mx
        l_ref[g] = alpha * l_ref[g] + jnp.sum(l, axis=0, keepdims=True)
        acc_ref[g] = alpha * acc_ref[g] + pv
        m_ref[g] = m_new

    def parity(c, g):
        return (g % 2) if group % 2 == 0 else (c * group + g) % 2

    mx0 = None
    for j in range(n_sub):
        mx0 = scores_sub(0, 0, 0, j, mx0)
    mx_ref[0] = mx0

    def chunk(c, carry):
        for g in range(group):
            if g + 1 < group:
                pair(c, g + 1, parity(c, g + 1), c, g, parity(c, g))
            else:
                pair(jnp.minimum(c + 1, n_chunks - 1), 0, parity(c + 1, 0), c, g, parity(c, g))
        return carry

    lax.fori_loop(0, n_chunks, chunk, 0, unroll=math.gcd(n_chunks, 4))
    for g in range(group):
        o = acc_ref[g] / l_ref[g]
        o_ref[:, g * HEAD_DIM:(g + 1) * HEAD_DIM] = o.T.astype(o_ref.dtype)


def _attention(qT, k, vT, *, row_start, batch, seq, kchunk):
    n_q = qT.shape[0]
    group = n_q // N_KV_HEADS
    tq = _pick(256, seq)
    nqt = seq // tq
    q0 = row_start // tq
    s0 = row_start // seq
    n_chunks = seq // kchunk
    kernel = functools.partial(_attention_kernel, n_chunks=n_chunks, kchunk=kchunk, ksub=_pick(256, kchunk),
                               group=group)
    return pl.pallas_call(
        kernel,
        grid=(batch, N_KV_HEADS, nqt),
        in_specs=[
            pl.BlockSpec((group, HEAD_DIM, tq), lambda b, h, i: (h, 0, q0 + b * nqt + i)),
            pl.BlockSpec((seq, HEAD_DIM), lambda b, h, i: (s0 + b, h)),
            pl.BlockSpec((1, n_chunks, HEAD_DIM, kchunk), lambda b, h, i: (h, s0 + b, 0, 0)),
        ],
        out_specs=pl.BlockSpec((tq, group * HEAD_DIM), lambda b, h, i: (b * nqt + i, h)),
        out_shape=jax.ShapeDtypeStruct((batch * seq, n_q * HEAD_DIM), BF16),
        scratch_shapes=[
            pltpu.VMEM((group, 1, tq), F32),
            pltpu.VMEM((group, 1, tq), F32),
            pltpu.VMEM((group, HEAD_DIM, tq), F32),
            pltpu.VMEM((2, kchunk, tq), F32),
            pltpu.VMEM((2, V7X_SUBLANES, tq), F32),
        ],
        compiler_params=_params(3),
        name="attention",
    )(qT, k, vT)


def _out_proj_kernel(xp_ref, xs_ref, gb_ref, u_ref, up_ref, un_ref, ap_ref, as_ref, cw_ref, cb_ref, wo_ref,
                     g2_ref, rw_ref, rb_ref,
                     x1_ref, h2_ref, mi_ref, mw_ref, cnt_ref, ubuf,
                     *, n_prompt_tiles, spt, sst, conv_w, tm):
    i = pl.program_id(0)
    is_p = i < n_prompt_tiles
    local = jnp.where(is_p, i % spt, (i - n_prompt_tiles) % sst)
    per_seq = jnp.where(is_p, spt, sst)
    keep_prev = (local > 0).astype(F32)
    keep_next = (local < per_seq - 1).astype(F32)

    sub = V7X_SUBLANES
    ubuf[0:sub, :] = up_ref[...] * keep_prev
    ubuf[sub:sub + tm, :] = u_ref[...]
    ubuf[sub + tm:2 * sub + tm, :] = un_ref[...] * keep_next
    conv = (ubuf[sub - 1:sub - 1 + tm, :] * cw_ref[0:1, :] + ubuf[sub:sub + tm, :] * cw_ref[1:2, :]
            + ubuf[sub + 1:sub + 1 + tm, :] * cw_ref[2:3, :] + cb_ref[...])
    conv_out = (gb_ref[...] * conv).astype(BF16)
    attn = jnp.where(is_p, ap_ref[...], as_ref[...])
    x = jnp.where(is_p, xp_ref[...], xs_ref[...])
    x1 = (x + jnp.dot(conv_out, wo_ref[0:conv_w, :], preferred_element_type=F32)
          + jnp.dot(attn, wo_ref[conv_w:, :], preferred_element_type=F32))
    x1_ref[...] = x1
    h2 = _rms(x1, g2_ref[...])
    h2_ref[...] = h2

    logits = jnp.dot(h2.astype(BF16), rw_ref[...], preferred_element_type=F32) + rb_ref[...]
    lane = lax.broadcasted_iota(jnp.int32, logits.shape, 1)
    vals, hots = [], []
    work = logits
    for _ in range(TOP_K):
        m = jnp.max(work, axis=-1, keepdims=True)
        idx = jnp.min(jnp.where(work == m, lane, V7X_LANES), axis=-1, keepdims=True)
        hot = lane == idx
        vals.append(m)
        hots.append((hot, idx))
        work = jnp.where(hot, -jnp.inf, work)
    es = [jnp.exp(v - vals[0]) for v in vals]
    den = es[0] + es[1] + es[2] + es[3]

    @pl.when(i == 0)
    def _():
        cnt_ref[...] = jnp.zeros(cnt_ref.shape, F32)

    sel = jnp.zeros(logits.shape, F32)
    for hot, _ in hots:
        sel = sel + hot.astype(F32)
    row = lax.broadcasted_iota(jnp.int32, (tm, tm), 0)
    col = lax.broadcasted_iota(jnp.int32, (tm, tm), 1)
    lower = jnp.where(col < row, 1.0, 0.0).astype(BF16)
    before = jnp.dot(lower, sel.astype(BF16), preferred_element_type=F32) + cnt_ref[0:1, :]
    mi = jnp.zeros(logits.shape, jnp.int32)
    mw = jnp.zeros(logits.shape, F32)
    for kk, (hot, idx) in enumerate(hots):
        rank = jnp.sum(jnp.where(hot, before, 0.0), axis=-1, keepdims=True).astype(jnp.int32)
        mi = jnp.where(lane == kk, idx, mi)
        mi = jnp.where(lane == TOP_K + kk, rank, mi)
        mw = jnp.where(lane == kk, es[kk] / den, mw)
    mi_ref[...] = mi
    mw_ref[...] = mw
    cnt_ref[...] = cnt_ref[...] + jnp.sum(sel, axis=0, keepdims=True)


def _out_proj(xp, xs, gb, u, attn_p, attn_s, conv_w_, conv_b, w_out, g2, rw, rb, *, s_p, s_s):
    n_p, d = xp.shape
    n_s = xs.shape[0]
    n = n_p + n_s
    conv_w = gb.shape[1]
    a_w = attn_p.shape[1]
    tm = _pick(256, math.gcd(s_p, s_s))
    npt, spt, sst = n_p // tm, s_p // tm, s_s // tm
    sub = V7X_SUBLANES
    hb = tm // sub
    kernel = functools.partial(_out_proj_kernel, n_prompt_tiles=npt, spt=spt, sst=sst, conv_w=conv_w, tm=tm)
    row = lambda i: (i, 0)
    return pl.pallas_call(
        kernel,
        grid=(n // tm,),
        in_specs=[
            pl.BlockSpec((tm, d), lambda i: (jnp.minimum(i, npt - 1), 0)),
            pl.BlockSpec((tm, d), lambda i: (jnp.maximum(i - npt, 0), 0)),
            pl.BlockSpec((tm, conv_w), row),
            pl.BlockSpec((tm, conv_w), row),
            pl.BlockSpec((sub, conv_w), lambda i: (jnp.maximum(i * hb - 1, 0), 0)),
            pl.BlockSpec((sub, conv_w), lambda i: (jnp.minimum((i + 1) * hb, n // sub - 1), 0)),
            pl.BlockSpec((tm, a_w), lambda i: (jnp.minimum(i, npt - 1), 0)),
            pl.BlockSpec((tm, a_w), lambda i: (jnp.maximum(i - npt, 0), 0)),
            _resident(conv_w_.shape),
            _resident((1, conv_w)),
            _resident(w_out.shape),
            _resident((1, d)),
            _resident(rw.shape),
            _resident(rb.shape),
        ],
        out_specs=[
            pl.BlockSpec((tm, d), row),
            pl.BlockSpec((tm, d), row),
            pl.BlockSpec((tm, V7X_LANES), row),
            pl.BlockSpec((tm, V7X_LANES), row),
            pl.BlockSpec((V7X_SUBLANES, V7X_LANES), lambda i: (0, 0)),
        ],
        out_shape=[
            jax.ShapeDtypeStruct((n, d), F32),
            jax.ShapeDtypeStruct((n, d), F32),
            jax.ShapeDtypeStruct((n, V7X_LANES), jnp.int32),
            jax.ShapeDtypeStruct((n, V7X_LANES), F32),
            jax.ShapeDtypeStruct((V7X_SUBLANES, V7X_LANES), F32),
        ],
        scratch_shapes=[pltpu.VMEM((tm + 2 * sub, conv_w), F32)],
        compiler_params=_params(1),
        name="out_proj",
    )(xp, xs, gb, u, u, u, attn_p, attn_s, conv_w_, conv_b, w_out, g2, rw, rb)


def _dispatch_kernel(pos_ref, h_hbm, xs_hbm, buf, sem_in, sem_out, *, tm, n_tiles):
    i = pl.program_id(0)
    slot = i % 2

    def load(tile, s):
        return pltpu.make_async_copy(h_hbm.at[pl.ds(tile * tm, tm)], buf.at[s], sem_in.at[s])

    def row_out(s, t, dst_row):
        return pltpu.make_async_copy(buf.at[s, pl.ds(t, 1)], xs_hbm.at[pl.ds(dst_row, 1)], sem_out.at[s])

    def drain(s):
        def body(t, carry):
            for _ in range(TOP_K):
                row_out(s, 0, 0).wait()
            return carry
        lax.fori_loop(0, tm, body, 0)

    @pl.when(i == 0)
    def _():
        load(0, 0).start()

    @pl.when(i > 0)
    def _():
        drain(1 - slot)

    @pl.when(i + 1 < n_tiles)
    def _():
        load(i + 1, 1 - slot).start()

    load(i, slot).wait()

    def issue(t, carry):
        for kk in range(TOP_K):
            row_out(slot, t, pos_ref[0, 0, t * TOP_K + kk]).start()
        return carry

    lax.fori_loop(0, tm, issue, 0, unroll=8)

    @pl.when(i == n_tiles - 1)
    def _():
        drain(slot)


def _dispatch(pos, h2):
    n, d = h2.shape
    tm = _pick(256, n)
    n_tiles = n // tm
    pos3 = pos.reshape(n_tiles, 1, tm * TOP_K)
    kernel = functools.partial(_dispatch_kernel, tm=tm, n_tiles=n_tiles)
    return pl.pallas_call(
        kernel,
        grid=(n_tiles,),
        in_specs=[
            pl.BlockSpec((1, 1, tm * TOP_K), lambda i: (i, 0, 0), memory_space=pltpu.SMEM),
            pl.BlockSpec(memory_space=pl.ANY),
        ],
        out_specs=pl.BlockSpec(memory_space=pl.ANY),
        out_shape=jax.ShapeDtypeStruct((n * TOP_K, d), h2.dtype),
        scratch_shapes=[pltpu.VMEM((2, tm, d), h2.dtype), pltpu.SemaphoreType.DMA((2,)),
                        pltpu.SemaphoreType.DMA((2,))],
        compiler_params=_params(1),
        name="dispatch",
    )(pos3, h2)


def _experts_kernel(vt_ref, ve_ref, vfirst_ref, vlo_ref, vhi_ref,
                    x_ref, wg_ref, wu_ref, bg_ref, bu_ref, wd_ref, bd_ref, o_ref, xb_ref):
    v = pl.program_id(0)
    f = pl.program_id(1)
    lo = vlo_ref[v]
    hi = vhi_ref[v]

    @pl.when(hi > lo)
    def _():
        @pl.when(f == 0)
        def _():
            xb_ref[...] = x_ref[...].astype(BF16)

        xb = xb_ref[...]
        gate = jnp.dot(xb, wg_ref[0], preferred_element_type=F32) + bg_ref[0]
        up = jnp.dot(xb, wu_ref[0], preferred_element_type=F32) + bu_ref[0]
        gate = jnp.minimum(gate, SWIGLU_LIMIT)
        up = jnp.clip(up, -SWIGLU_LIMIT, SWIGLU_LIMIT)
        glu = gate * jax.nn.sigmoid(gate * SWIGLU_ALPHA)
        rows = lax.broadcasted_iota(jnp.int32, (xb.shape[0], 1), 0)
        mine = (rows >= lo) & (rows < hi)
        act = jnp.where(mine, (up + 1.0) * glu, 0.0).astype(BF16)
        y = jnp.dot(act, wd_ref[0], preferred_element_type=F32)
        bias_on = jnp.where(f == 0, 1.0, 0.0)
        y = y + jnp.where(mine, bd_ref[0] * bias_on, 0.0)
        fresh = jnp.logical_and(vfirst_ref[v] == 1, f == 0)

        @pl.when(fresh)
        def _():
            o_ref[...] = y

        @pl.when(jnp.logical_not(fresh))
        def _():
            o_ref[...] = o_ref[...] + y


def _visit_plan(counts, n_rows, tm, n_visits):
    n_exp = counts.shape[0]
    ends = jnp.cumsum(counts)
    starts = ends - counts
    first_tile = starts // tm
    last_tile = jnp.maximum(ends - 1, 0) // tm
    nvis = jnp.where(counts > 0, last_tile - first_tile + 1, 0)
    vis_end = jnp.cumsum(nvis)
    vis_start = vis_end - nvis
    total = vis_end[-1]
    v = jnp.arange(n_visits, dtype=jnp.int32)
    e = jnp.sum((v[:, None] >= vis_end[None, :]).astype(jnp.int32), axis=1)
    e = jnp.minimum(e, n_exp - 1)
    valid = v < total
    onehot = (e[:, None] == jnp.arange(n_exp, dtype=jnp.int32)[None, :]).astype(jnp.int32)
    pickv = lambda a: jnp.sum(onehot * a[None, :], axis=1)
    tile = pickv(first_tile) + (v - pickv(vis_start))
    lo = jnp.maximum(pickv(starts), tile * tm) - tile * tm
    hi = jnp.minimum(pickv(ends), (tile + 1) * tm) - tile * tm
    last_e = jnp.sum((total - 1 >= vis_end).astype(jnp.int32))
    tile = jnp.where(valid, tile, n_rows // tm - 1)
    e = jnp.where(valid, e, jnp.minimum(last_e, n_exp - 1))
    lo = jnp.where(valid, lo, 0)
    hi = jnp.where(valid, hi, 0)
    first = jnp.where(valid & (lo == 0), 1, 0)
    i32 = lambda a: a.astype(jnp.int32)
    return i32(tile), i32(e), i32(first), i32(lo), i32(hi)


def _experts(plan, xs, w_gu, b_gu, w_d, b_d):
    n_rows, d = xs.shape
    n_exp, _, two_f = w_gu.shape
    d_ff = two_f // 2
    tm = _pick(512, n_rows)
    tf = _pick(1024, d_ff)
    nf = d_ff // tf
    n_visits = plan[0].shape[0]

    def fx(f, hi, lo, v):
        return jnp.where(hi[v] > lo[v], f, 0)

    grid_spec = pltpu.PrefetchScalarGridSpec(
        num_scalar_prefetch=5,
        grid=(n_visits, nf),
        in_specs=[
            pl.BlockSpec((tm, d), lambda v, f, vt, ve, vf, lo, hi: (vt[v], 0)),
            pl.BlockSpec((1, d, tf), lambda v, f, vt, ve, vf, lo, hi: (ve[v], 0, fx(f, hi, lo, v))),
            pl.BlockSpec((1, d, tf), lambda v, f, vt, ve, vf, lo, hi: (ve[v], 0, nf + fx(f, hi, lo, v))),
            pl.BlockSpec((1, 1, tf), lambda v, f, vt, ve, vf, lo, hi: (ve[v], 0, fx(f, hi, lo, v))),
            pl.BlockSpec((1, 1, tf), lambda v, f, vt, ve, vf, lo, hi: (ve[v], 0, nf + fx(f, hi, lo, v))),
            pl.BlockSpec((1, tf, d), lambda v, f, vt, ve, vf, lo, hi: (ve[v], fx(f, hi, lo, v), 0)),
            pl.BlockSpec((1, 1, d), lambda v, f, vt, ve, vf, lo, hi: (ve[v], 0, 0)),
        ],
        out_specs=pl.BlockSpec((tm, d), lambda v, f, vt, ve, vf, lo, hi: (vt[v], 0)),
        scratch_shapes=[pltpu.VMEM((tm, d), BF16)],
    )
    return pl.pallas_call(
        _experts_kernel,
        grid_spec=grid_spec,
        out_shape=jax.ShapeDtypeStruct((n_rows, d), F32),
        compiler_params=_params(2),
        name="experts",
    )(*plan, xs, w_gu, w_gu, b_gu, b_gu, w_d, b_d)


def _combine_kernel(pos_ref, nxt_ref, y_hbm, x1_ref, w_ref, fg_ref, o_ref, buf, sem, *, tm, n_steps):
    i = pl.program_id(0)
    slot = i % 2

    def fetch(p_ref, s):
        def body(t, carry):
            for kk in range(TOP_K):
                pltpu.make_async_copy(y_hbm.at[pl.ds(p_ref[0, 0, t * TOP_K + kk], 1)],
                                      buf.at[s, kk, pl.ds(t, 1)], sem.at[s]).start()
            return carry
        lax.fori_loop(0, tm, body, 0, unroll=8)

    @pl.when(i == 0)
    def _():
        fetch(pos_ref, 0)

    @pl.when(i + 1 < n_steps)
    def _():
        fetch(nxt_ref, 1 - slot)

    def drain(t, carry):
        for kk in range(TOP_K):
            pltpu.make_async_copy(y_hbm.at[pl.ds(0, 1)], buf.at[slot, kk, pl.ds(t, 1)], sem.at[slot]).wait()
        return carry

    lax.fori_loop(0, tm, drain, 0)
    acc = x1_ref[...]
    for kk in range(TOP_K):
        acc = acc + w_ref[:, kk:kk + 1] * buf[slot, kk]
    o_ref[...] = _rms(acc, fg_ref[...])


def _combine(pos, y_sorted, x1, wts, fg, *, row_start, rows):
    n, d = x1.shape
    tm = _pick(128, rows)
    n_steps = rows // tm
    t0 = row_start // tm
    pos3 = pos.reshape(n // tm, 1, tm * TOP_K)
    kernel = functools.partial(_combine_kernel, tm=tm, n_steps=n_steps)
    smem = lambda im: pl.BlockSpec((1, 1, tm * TOP_K), im, memory_space=pltpu.SMEM)
    return pl.pallas_call(
        kernel,
        grid=(n_steps,),
        in_specs=[
            smem(lambda i: (t0 + i, 0, 0)),
            smem(lambda i: (t0 + jnp.minimum(i + 1, n_steps - 1), 0, 0)),
            pl.BlockSpec(memory_space=pl.ANY),
            pl.BlockSpec((tm, d), lambda i: (t0 + i, 0)),
            pl.BlockSpec((tm, V7X_LANES), lambda i: (t0 + i, 0)),
            _resident((1, d)),
        ],
        out_specs=pl.BlockSpec((tm, d), lambda i: (i, 0)),
        out_shape=jax.ShapeDtypeStruct((rows, d), F32),
        scratch_shapes=[pltpu.VMEM((2, TOP_K, tm, d), F32), pltpu.SemaphoreType.DMA((2,))],
        compiler_params=_params(1),
        name="combine",
    )(pos3, pos3, y_sorted, x1, wts, fg)


def _rope_tables(n_tokens):
    rows = n_tokens // GRID_W
    axis_dim = HEAD_DIM // 2
    row = jnp.repeat(jnp.arange(rows, dtype=F32), GRID_W)
    col = jnp.tile(jnp.arange(GRID_W, dtype=F32), rows)
    inv_freq = ROPE_THETA ** (-jnp.arange(0, axis_dim, 2, dtype=F32) / axis_dim)
    ang_r = row[:, None] * inv_freq[None, :]
    ang_c = col[:, None] * inv_freq[None, :]
    ang = jnp.concatenate([ang_r, ang_r, ang_c, ang_c], axis=-1)
    quarter = jnp.arange(HEAD_DIM) % (HEAD_DIM // 2) < HEAD_DIM // 4
    return jnp.cos(ang), jnp.where(quarter[None, :], -1.0, 1.0).astype(F32) * jnp.sin(ang)


def kernel(x_prompt, x_sample, norm1_g, w_in, conv_w, conv_b, q_norm_g, k_norm_g, w_out, norm2_g,
           router_w, router_b, w_gate_up, b_gate_up, w_down, b_down, final_g):
    assert norm1_g.shape[0] == 1, "one layer"
    b_p, s_p, d = x_prompt.shape
    b_s, s_s, _ = x_sample.shape
    n_p, n_s = b_p * s_p, b_s * s_s
    n = n_p + n_s
    n_exp = router_w.shape[-1]
    assert n_exp <= V7X_LANES and n_p % s_s == 0
    kchunk = _pick(1024, math.gcd(s_p, s_s))
    xp = x_prompt.reshape(n_p, d)
    xs = x_sample.reshape(n_s, d)
    cos, sin_signed = _rope_tables(max(s_p, s_s))

    gb, u, qT, k, vT = _in_proj(
        xp, xs, norm1_g[0][None, :], w_in[0].astype(BF16), q_norm_g[0][None, :], k_norm_g[0][None, :],
        cos, sin_signed, s_p=s_p, s_s=s_s, kchunk=kchunk)
    attn_p = _attention(qT, k, vT, row_start=0, batch=b_p, seq=s_p, kchunk=kchunk)
    attn_s = _attention(qT, k, vT, row_start=n_p, batch=b_s, seq=s_s, kchunk=kchunk)

    rw = jnp.zeros((d, V7X_LANES), BF16).at[:, :n_exp].set(router_w[0].astype(BF16))
    rb = jnp.full((1, V7X_LANES), NEG_PAD, F32).at[0, :n_exp].set(router_b[0])
    x1, h2, meta_i, meta_w, cnt = _out_proj(
        xp, xs, gb, u, attn_p, attn_s, conv_w[0], conv_b[0][None, :], w_out[0].astype(BF16),
        norm2_g[0][None, :], rw, rb, s_p=s_p, s_s=s_s)

    counts = cnt[0, :n_exp].astype(jnp.int32)
    starts = jnp.cumsum(counts) - counts
    ids = meta_i[:, :TOP_K]
    ranks = meta_i[:, TOP_K:2 * TOP_K]
    onehot = ids[:, :, None] == jnp.arange(n_exp, dtype=jnp.int32)[None, None, :]
    pos = ranks + jnp.sum(jnp.where(onehot, starts[None, None, :], 0), axis=-1)

    x_sorted = _dispatch(pos, h2)
    n_rows = n * TOP_K
    tm_e = _pick(512, n_rows)
    plan = _visit_plan(counts, n_rows, tm_e, n_rows // tm_e + n_exp - 1)
    y_sorted = _experts(plan, x_sorted, w_gate_up[0].astype(BF16), b_gate_up[0][:, None, :],
                        w_down[0].astype(BF16), b_down[0][:, None, :])

    fg = final_g[None, :]
    y_p = _combine(pos, y_sorted, x1, meta_w, fg, row_start=0, rows=n_p)
    y_s = _combine(pos, y_sorted, x1, meta_w, fg, row_start=n_p, rows=n_s)
    return (y_p.reshape(b_p, s_p, d), y_s.reshape(b_s, s_s, d))
```

```python
import functools
import math

import jax
import jax.numpy as jnp
from jax import lax
from jax.experimental import pallas as pl
from jax.experimental.pallas import tpu as pltpu

HEAD_DIM = 128
N_KV_HEADS = 2
GRID_W = 64
ROPE_THETA = 10000.0
TOP_K = 4
SWIGLU_LIMIT = 7.0
SWIGLU_ALPHA = 1.702
NORM_EPS = 1e-6

V7X_LANES = 128
V7X_SUBLANES = 8
V7X_VMEM_BYTES = 64 * 1024 * 1024
VMEM_LIMIT = V7X_VMEM_BYTES - 8 * 1024 * 1024
NEG_PAD = -1e30

F32 = jnp.float32
BF16 = jnp.bfloat16
LOG2E = math.log2(math.e)


def _pick(target, n):
    t = min(target, n)
    while n % t:
        t //= 2
    return t


def _params(n_axes, vmem=VMEM_LIMIT):
    return pltpu.CompilerParams(dimension_semantics=("arbitrary",) * n_axes, vmem_limit_bytes=vmem)


def _resident(shape):
    nd = len(shape)
    return pl.BlockSpec(shape, lambda *_: (0,) * nd, pipeline_mode=pl.Buffered(1))


def _rms(x, g):
    ms = jnp.mean(x * x, axis=-1, keepdims=True)
    return x * lax.rsqrt(ms + NORM_EPS) * g


def _in_proj_kernel(xp_ref, xs_ref, g1_ref, w_ref, qg_ref, kg_ref, cos_ref, sin_ref,
                    gb_ref, u_ref, qT_ref, k_ref, vT_ref, *, n_prompt_tiles, conv_w, n_q, q_scale):
    i = pl.program_id(0)
    x = jnp.where(i < n_prompt_tiles, xp_ref[...], xs_ref[...])
    h = _rms(x, g1_ref[...]).astype(BF16)

    def proj(c0, c1):
        return jnp.dot(h, w_ref[:, c0:c1], preferred_element_type=F32)

    gb_ref[...] = proj(0, conv_w)
    u_ref[...] = proj(conv_w, 2 * conv_w) * proj(2 * conv_w, 3 * conv_w)

    cos = cos_ref[...]
    sin = sin_ref[...]
    lane = lax.broadcasted_iota(jnp.int32, (1, HEAD_DIM), 1)
    first_half = (lane % (HEAD_DIM // 2)) < (HEAD_DIM // 4)

    def norm_rope(xh, g):
        y = _rms(xh, g)
        up = pltpu.roll(y, HEAD_DIM - HEAD_DIM // 4, 1)
        dn = pltpu.roll(y, HEAD_DIM // 4, 1)
        return y * cos + jnp.where(first_half, up, dn) * sin

    q0 = 3 * conv_w
    q = proj(q0, q0 + n_q * HEAD_DIM)
    for hd in range(n_q):
        qh = norm_rope(q[:, hd * HEAD_DIM:(hd + 1) * HEAD_DIM], qg_ref[...]) * q_scale
        qT_ref[hd] = qh.T.astype(BF16)
    k0 = q0 + n_q * HEAD_DIM
    kv = proj(k0, k0 + 2 * N_KV_HEADS * HEAD_DIM)
    for hd in range(N_KV_HEADS):
        kh = norm_rope(kv[:, hd * HEAD_DIM:(hd + 1) * HEAD_DIM], kg_ref[...])
        k_ref[:, hd * HEAD_DIM:(hd + 1) * HEAD_DIM] = kh.astype(BF16)
        vh = kv[:, (N_KV_HEADS + hd) * HEAD_DIM:(N_KV_HEADS + hd + 1) * HEAD_DIM]
        vT_ref[hd, 0] = vh.T.astype(BF16)


def _in_proj(xp, xs, g1, w_in, qg, kg, cos, sin_signed, *, s_p, s_s, kchunk):
    n_p, d = xp.shape
    n_s = xs.shape[0]
    n = n_p + n_s
    conv_w = d // 2
    n_q = (d - conv_w) // HEAD_DIM
    tm = _pick(256, math.gcd(s_p, s_s))
    npt, spt, sst = n_p // tm, s_p // tm, s_s // tm
    r = kchunk // tm
    q_scale = LOG2E / math.sqrt(HEAD_DIM)

    def pos_map(i):
        return (jnp.where(i < npt, i % spt, (i - npt) % sst), 0)

    kernel = functools.partial(_in_proj_kernel, n_prompt_tiles=npt, conv_w=conv_w, n_q=n_q, q_scale=q_scale)
    return pl.pallas_call(
        kernel,
        grid=(n // tm,),
        in_specs=[
            pl.BlockSpec((tm, d), lambda i: (jnp.minimum(i, npt - 1), 0)),
            pl.BlockSpec((tm, d), lambda i: (jnp.maximum(i - npt, 0), 0)),
            _resident((1, d)),
            _resident(w_in.shape),
            _resident((1, HEAD_DIM)),
            _resident((1, HEAD_DIM)),
            pl.BlockSpec((tm, HEAD_DIM), pos_map),
            pl.BlockSpec((tm, HEAD_DIM), pos_map),
        ],
        out_specs=[
            pl.BlockSpec((tm, conv_w), lambda i: (i, 0)),
            pl.BlockSpec((tm, conv_w), lambda i: (i, 0)),
            pl.BlockSpec((n_q, HEAD_DIM, tm), lambda i: (0, 0, i)),
            pl.BlockSpec((tm, N_KV_HEADS * HEAD_DIM), lambda i: (i, 0)),
            pl.BlockSpec((N_KV_HEADS, 1, HEAD_DIM, tm), lambda i: (0, i // r, 0, i % r)),
        ],
        out_shape=[
            jax.ShapeDtypeStruct((n, conv_w), F32),
            jax.ShapeDtypeStruct((n, conv_w), F32),
            jax.ShapeDtypeStruct((n_q, HEAD_DIM, n), BF16),
            jax.ShapeDtypeStruct((n, N_KV_HEADS * HEAD_DIM), BF16),
            jax.ShapeDtypeStruct((N_KV_HEADS, n // kchunk, HEAD_DIM, kchunk), BF16),
        ],
        compiler_params=_params(1),
        name="in_proj",
    )(xp, xs, g1, w_in, qg, kg, cos, sin_signed)


def _attention_kernel(qT_ref, k_ref, vT_ref, w_ref, o_ref, wb_ref, m_ref, l_ref, acc_ref, s_ref, mx_ref,
                      *, n_chunks, kchunk, ksub, group):
    wb_ref[...] = w_ref[...].astype(BF16)

    m_ref[...] = jnp.full(m_ref.shape, -jnp.inf, F32)
    l_ref[...] = jnp.zeros(l_ref.shape, F32)
    acc_ref[...] = jnp.zeros(acc_ref.shape, F32)
    sub = V7X_SUBLANES

    n_sub = kchunk // ksub

    def scores_sub(c, g, par, j, mx):
        r0 = pl.multiple_of(c * kchunk + j * ksub, ksub)
        sT = jnp.dot(k_ref[pl.ds(r0, ksub), :], qT_ref[g], preferred_element_type=F32)
        s_ref[par, j * ksub:(j + 1) * ksub, :] = sT
        cm = jnp.max(sT.reshape(ksub // sub, sub, sT.shape[1]), axis=0)
        return cm if mx is None else jnp.maximum(mx, cm)

    def consume_sub(c, par, j, m_new, pv, l):
        p = jnp.exp2(s_ref[par, j * ksub:(j + 1) * ksub, :] - m_new)
        d = jnp.dot(vT_ref[0, c, :, j * ksub:(j + 1) * ksub], p.astype(BF16), preferred_element_type=F32)
        ps = jnp.sum(p.reshape(ksub // sub, sub, p.shape[1]), axis=0)
        return (d if pv is None else pv + d), (ps if l is None else l + ps)

    def pair(c_next, g_next, par_next, c, g, par):
        m_old = m_ref[g]
        m_new = jnp.maximum(m_old, jnp.max(mx_ref[par], axis=0, keepdims=True))
        alpha = jnp.exp2(m_old - m_new)
        mx = pv = l = None
        for j in range(n_sub):
            mx = scores_sub(c_next, g_next, par_next, j, mx)
            pv, l = consume_sub(c, par, j, m_new, pv, l)
        mx_ref[par_next] = mx
        l_ref[g] = alpha * l_ref[g] + jnp.sum(l, axis=0, keepdims=True)
        acc_ref[g] = alpha * acc_ref[g] + pv
        m_ref[g] = m_new

    def parity(c, g):
        return (g % 2) if group % 2 == 0 else (c * group + g) % 2

    mx0 = None
    for j in range(n_sub):
        mx0 = scores_sub(0, 0, 0, j, mx0)
    mx_ref[0] = mx0

    def chunk(c, carry):
        for g in range(group):
            if g + 1 < group:
                pair(c, g + 1, parity(c, g + 1), c, g, parity(c, g))
            else:
                pair(jnp.minimum(c + 1, n_chunks - 1), 0, parity(c + 1, 0), c, g, parity(c, g))
        return carry

    lax.fori_loop(0, n_chunks, chunk, 0, unroll=math.gcd(n_chunks, 4))
    for g in range(group):
        o = acc_ref[g] / l_ref[g]
        o_ref[:, g * HEAD_DIM:(g + 1) * HEAD_DIM] = o.T.astype(o_ref.dtype)


def _attention(qT, k, vT, w, *, row_start, batch, seq, kchunk):
    n_q = qT.shape[0]
    group = n_q // N_KV_HEADS
    tq = _pick(256, seq)
    nqt = seq // tq
    q0 = row_start // tq
    s0 = row_start // seq
    n_chunks = seq // kchunk
    n_steps = batch * N_KV_HEADS * nqt
    w_rows, w_cols = w.shape
    wr = w_rows // n_steps
    assert wr * n_steps == w_rows and wr % (2 * V7X_SUBLANES) == 0, "weight rows must split evenly over the grid"
    step = lambda b, h, i: ((b * N_KV_HEADS + h) * nqt + i, 0)
    kernel = functools.partial(_attention_kernel, n_chunks=n_chunks, kchunk=kchunk, ksub=_pick(256, kchunk),
                               group=group)
    return pl.pallas_call(
        kernel,
        grid=(batch, N_KV_HEADS, nqt),
        in_specs=[
            pl.BlockSpec((group, HEAD_DIM, tq), lambda b, h, i: (h, 0, q0 + b * nqt + i)),
            pl.BlockSpec((seq, HEAD_DIM), lambda b, h, i: (s0 + b, h)),
            pl.BlockSpec((1, n_chunks, HEAD_DIM, kchunk), lambda b, h, i: (h, s0 + b, 0, 0)),
            pl.BlockSpec((wr, w_cols), step),
        ],
        out_specs=[
            pl.BlockSpec((tq, group * HEAD_DIM), lambda b, h, i: (b * nqt + i, h)),
            pl.BlockSpec((wr, w_cols), step),
        ],
        out_shape=[
            jax.ShapeDtypeStruct((batch * seq, n_q * HEAD_DIM), BF16),
            jax.ShapeDtypeStruct((w_rows, w_cols), BF16),
        ],
        scratch_shapes=[
            pltpu.VMEM((group, 1, tq), F32),
            pltpu.VMEM((group, 1, tq), F32),
            pltpu.VMEM((group, HEAD_DIM, tq), F32),
            pltpu.VMEM((2, kchunk, tq), F32),
            pltpu.VMEM((2, V7X_SUBLANES, tq), F32),
        ],
        compiler_params=_params(3),
        name="attention",
    )(qT, k, vT, w)


def _out_proj_kernel(xp_ref, xs_ref, gb_ref, u_ref, up_ref, un_ref, ap_ref, as_ref, cw_ref, cb_ref, wo_ref,
                     g2_ref, rw_ref, rb_ref,
                     x1_ref, h2_ref, mi_ref, mw_ref, cnt_ref, ubuf,
                     *, n_prompt_tiles, spt, sst, conv_w, tm):
    i = pl.program_id(0)
    is_p = i < n_prompt_tiles
    local = jnp.where(is_p, i % spt, (i - n_prompt_tiles) % sst)
    per_seq = jnp.where(is_p, spt, sst)
    keep_prev = (local > 0).astype(F32)
    keep_next = (local < per_seq - 1).astype(F32)

    sub = V7X_SUBLANES
    ubuf[0:sub, :] = up_ref[...] * keep_prev
    ubuf[sub:sub + tm, :] = u_ref[...]
    ubuf[sub + tm:2 * sub + tm, :] = un_ref[...] * keep_next
    conv = (ubuf[sub - 1:sub - 1 + tm, :] * cw_ref[0:1, :] + ubuf[sub:sub + tm, :] * cw_ref[1:2, :]
            + ubuf[sub + 1:sub + 1 + tm, :] * cw_ref[2:3, :] + cb_ref[...])
    conv_out = (gb_ref[...] * conv).astype(BF16)
    attn = jnp.where(is_p, ap_ref[...], as_ref[...])
    x = jnp.where(is_p, xp_ref[...], xs_ref[...])
    x1 = (x + jnp.dot(conv_out, wo_ref[0:conv_w, :], preferred_element_type=F32)
          + jnp.dot(attn, wo_ref[conv_w:, :], preferred_element_type=F32))
    x1_ref[...] = x1
    h2 = _rms(x1, g2_ref[...])
    h2_ref[...] = h2

    logits = jnp.dot(h2.astype(BF16), rw_ref[...], preferred_element_type=F32) + rb_ref[...]
    lane = lax.broadcasted_iota(jnp.int32, logits.shape, 1)
    vals, hots = [], []
    work = logits
    for _ in range(TOP_K):
        m = jnp.max(work, axis=-1, keepdims=True)
        idx = jnp.min(jnp.where(work == m, lane, V7X_LANES), axis=-1, keepdims=True)
        hot = lane == idx
        vals.append(m)
        hots.append((hot, idx))
        work = jnp.where(hot, -jnp.inf, work)
    es = [jnp.exp(v - vals[0]) for v in vals]
    den = es[0] + es[1] + es[2] + es[3]

    @pl.when(i == 0)
    def _():
        cnt_ref[...] = jnp.zeros(cnt_ref.shape, F32)

    sel = jnp.zeros(logits.shape, F32)
    for hot, _ in hots:
        sel = sel + hot.astype(F32)
    row = lax.broadcasted_iota(jnp.int32, (tm, tm), 0)
    col = lax.broadcasted_iota(jnp.int32, (tm, tm), 1)
    lower = jnp.where(col < row, 1.0, 0.0).astype(BF16)
    before = jnp.dot(lower, sel.astype(BF16), preferred_element_type=F32) + cnt_ref[0:1, :]
    mi = jnp.zeros(logits.shape, jnp.int32)
    mw = jnp.zeros(logits.shape, F32)
    for kk, (hot, idx) in enumerate(hots):
        rank = jnp.sum(jnp.where(hot, before, 0.0), axis=-1, keepdims=True).astype(jnp.int32)
        mi = jnp.where(lane == kk, idx, mi)
        mi = jnp.where(lane == TOP_K + kk, rank, mi)
        mw = jnp.where(lane == kk, es[kk] / den, mw)
    mi_ref[...] = mi
    mw_ref[...] = mw
    cnt_ref[...] = cnt_ref[...] + jnp.sum(sel, axis=0, keepdims=True)


def _out_proj(xp, xs, gb, u, attn_p, attn_s, conv_w_, conv_b, w_out, g2, rw, rb, *, s_p, s_s):
    n_p, d = xp.shape
    n_s = xs.shape[0]
    n = n_p + n_s
    conv_w = gb.shape[1]
    a_w = attn_p.shape[1]
    tm = _pick(256, math.gcd(s_p, s_s))
    npt, spt, sst = n_p // tm, s_p // tm, s_s // tm
    sub = V7X_SUBLANES
    hb = tm // sub
    kernel = functools.partial(_out_proj_kernel, n_prompt_tiles=npt, spt=spt, sst=sst, conv_w=conv_w, tm=tm)
    row = lambda i: (i, 0)
    return pl.pallas_call(
        kernel,
        grid=(n // tm,),
        in_specs=[
            pl.BlockSpec((tm, d), lambda i: (jnp.minimum(i, npt - 1), 0)),
            pl.BlockSpec((tm, d), lambda i: (jnp.maximum(i - npt, 0), 0)),
            pl.BlockSpec((tm, conv_w), row),
            pl.BlockSpec((tm, conv_w), row),
            pl.BlockSpec((sub, conv_w), lambda i: (jnp.maximum(i * hb - 1, 0), 0)),
            pl.BlockSpec((sub, conv_w), lambda i: (jnp.minimum((i + 1) * hb, n // sub - 1), 0)),
            pl.BlockSpec((tm, a_w), lambda i: (jnp.minimum(i, npt - 1), 0)),
            pl.BlockSpec((tm, a_w), lambda i: (jnp.maximum(i - npt, 0), 0)),
            _resident(conv_w_.shape),
            _resident((1, conv_w)),
            _resident(w_out.shape),
            _resident((1, d)),
            _resident(rw.shape),
            _resident(rb.shape),
        ],
        out_specs=[
            pl.BlockSpec((tm, d), row),
            pl.BlockSpec((tm, d), row),
            pl.BlockSpec((tm, V7X_LANES), row),
            pl.BlockSpec((tm, V7X_LANES), row),
            pl.BlockSpec((V7X_SUBLANES, V7X_LANES), lambda i: (0, 0)),
        ],
        out_shape=[
            jax.ShapeDtypeStruct((n, d), F32),
            jax.ShapeDtypeStruct((n, d), F32),
            jax.ShapeDtypeStruct((n, V7X_LANES), jnp.int32),
            jax.ShapeDtypeStruct((n, V7X_LANES), F32),
            jax.ShapeDtypeStruct((V7X_SUBLANES, V7X_LANES), F32),
        ],
        scratch_shapes=[pltpu.VMEM((tm + 2 * sub, conv_w), F32)],
        compiler_params=_params(1),
        name="out_proj",
    )(xp, xs, gb, u, u, u, attn_p, attn_s, conv_w_, conv_b, w_out, g2, rw, rb)


def _dispatch_kernel(pos_ref, h_hbm, xs_hbm, buf, sem_in, sem_out, *, tm, n_tiles):
    i = pl.program_id(0)
    slot = i % 2

    def load(tile, s):
        return pltpu.make_async_copy(h_hbm.at[pl.ds(tile * tm, tm)], buf.at[s], sem_in.at[s])

    def row_out(s, t, dst_row):
        return pltpu.make_async_copy(buf.at[s, pl.ds(t, 1)], xs_hbm.at[pl.ds(dst_row, 1)], sem_out.at[s])

    def drain(s):
        for _ in range(TOP_K):
            pltpu.make_async_copy(buf.at[s], xs_hbm.at[pl.ds(0, tm)], sem_out.at[s]).wait()

    @pl.when(i == 0)
    def _():
        load(0, 0).start()

    @pl.when(i > 0)
    def _():
        drain(1 - slot)

    @pl.when(i + 1 < n_tiles)
    def _():
        load(i + 1, 1 - slot).start()

    load(i, slot).wait()

    def issue(t, carry):
        for kk in range(TOP_K):
            row_out(slot, t, pos_ref[0, 0, t * TOP_K + kk]).start()
        return carry

    lax.fori_loop(0, tm, issue, 0, unroll=8)

    @pl.when(i == n_tiles - 1)
    def _():
        drain(slot)


def _dispatch(pos, h2):
    n, d = h2.shape
    tm = _pick(256, n)
    n_tiles = n // tm
    pos3 = pos.reshape(n_tiles, 1, tm * TOP_K)
    kernel = functools.partial(_dispatch_kernel, tm=tm, n_tiles=n_tiles)
    return pl.pallas_call(
        kernel,
        grid=(n_tiles,),
        in_specs=[
            pl.BlockSpec((1, 1, tm * TOP_K), lambda i: (i, 0, 0), memory_space=pltpu.SMEM),
            pl.BlockSpec(memory_space=pl.ANY),
        ],
        out_specs=pl.BlockSpec(memory_space=pl.ANY),
        out_shape=jax.ShapeDtypeStruct((n * TOP_K, d), h2.dtype),
        scratch_shapes=[pltpu.VMEM((2, tm, d), h2.dtype), pltpu.SemaphoreType.DMA((2,)),
                        pltpu.SemaphoreType.DMA((2,))],
        compiler_params=_params(1),
        name="dispatch",
    )(pos3, h2)


def _experts_kernel(vt_ref, ve_ref, vfirst_ref, vlo_ref, vhi_ref,
                    x_ref, wg_ref, wu_ref, bg_ref, bu_ref, wd_ref, bd_ref, o_ref, act_ref, *, nf, tf):
    v = pl.program_id(0)
    f = pl.program_id(1)
    lo = vlo_ref[v]
    hi = vhi_ref[v]

    @pl.when(hi > lo)
    def _():
        xb = x_ref[...].astype(BF16)
        gate = jnp.dot(xb, wg_ref[0], preferred_element_type=F32) + bg_ref[0]
        up = jnp.dot(xb, wu_ref[0], preferred_element_type=F32) + bu_ref[0]
        gate = jnp.minimum(gate, SWIGLU_LIMIT)
        up = jnp.clip(up, -SWIGLU_LIMIT, SWIGLU_LIMIT)
        glu = gate * jax.nn.sigmoid(gate * SWIGLU_ALPHA)
        rows = lax.broadcasted_iota(jnp.int32, (xb.shape[0], 1), 0)
        mine = (rows >= lo) & (rows < hi)
        act_ref[f] = jnp.where(mine, (up + 1.0) * glu, 0.0).astype(BF16)

        def down():
            y = jnp.dot(act_ref[0], wd_ref[0, 0:tf, :], preferred_element_type=F32)
            for j in range(1, nf):
                y = y + jnp.dot(act_ref[j], wd_ref[0, j * tf:(j + 1) * tf, :], preferred_element_type=F32)
            return y + jnp.where(mine, bd_ref[0], 0.0)

        last = f == nf - 1
        fresh = vfirst_ref[v] == 1

        @pl.when(jnp.logical_and(last, fresh))
        def _():
            o_ref[...] = down()

        @pl.when(jnp.logical_and(last, jnp.logical_not(fresh)))
        def _():
            o_ref[...] = o_ref[...] + down()


def _visit_plan(counts, n_rows, tm, n_visits):
    n_exp = counts.shape[0]
    ends = jnp.cumsum(counts)
    starts = ends - counts
    first_tile = starts // tm
    last_tile = jnp.maximum(ends - 1, 0) // tm
    nvis = jnp.where(counts > 0, last_tile - first_tile + 1, 0)
    vis_end = jnp.cumsum(nvis)
    vis_start = vis_end - nvis
    total = vis_end[-1]
    v = jnp.arange(n_visits, dtype=jnp.int32)
    e = jnp.sum((v[:, None] >= vis_end[None, :]).astype(jnp.int32), axis=1)
    e = jnp.minimum(e, n_exp - 1)
    valid = v < total
    onehot = (e[:, None] == jnp.arange(n_exp, dtype=jnp.int32)[None, :]).astype(jnp.int32)
    pickv = lambda a: jnp.sum(onehot * a[None, :], axis=1)
    tile = pickv(first_tile) + (v - pickv(vis_start))
    lo = jnp.maximum(pickv(starts), tile * tm) - tile * tm
    hi = jnp.minimum(pickv(ends), (tile + 1) * tm) - tile * tm
    last_e = jnp.sum((total - 1 >= vis_end).astype(jnp.int32))
    tile = jnp.where(valid, tile, n_rows // tm - 1)
    e = jnp.where(valid, e, jnp.minimum(last_e, n_exp - 1))
    lo = jnp.where(valid, lo, 0)
    hi = jnp.where(valid, hi, 0)
    first = jnp.where(valid & (lo == 0), 1, 0)
    i32 = lambda a: a.astype(jnp.int32)
    return i32(tile), i32(e), i32(first), i32(lo), i32(hi)


def _experts(plan, xs, w_gu, b_gu, w_d, b_d):
    n_rows, d = xs.shape
    n_exp, _, two_f = w_gu.shape
    d_ff = two_f // 2
    tm = _pick(512, n_rows)
    tf = _pick(1024, d_ff)
    nf = d_ff // tf
    n_visits = plan[0].shape[0]

    def fx(f, hi, lo, v):
        return jnp.where(hi[v] > lo[v], f, 0)

    grid_spec = pltpu.PrefetchScalarGridSpec(
        num_scalar_prefetch=5,
        grid=(n_visits, nf),
        in_specs=[
            pl.BlockSpec((tm, d), lambda v, f, vt, ve, vf, lo, hi: (vt[v], 0)),
            pl.BlockSpec((1, d, tf), lambda v, f, vt, ve, vf, lo, hi: (ve[v], 0, fx(f, hi, lo, v))),
            pl.BlockSpec((1, d, tf), lambda v, f, vt, ve, vf, lo, hi: (ve[v], 0, nf + fx(f, hi, lo, v))),
            pl.BlockSpec((1, 1, tf), lambda v, f, vt, ve, vf, lo, hi: (ve[v], 0, fx(f, hi, lo, v))),
            pl.BlockSpec((1, 1, tf), lambda v, f, vt, ve, vf, lo, hi: (ve[v], 0, nf + fx(f, hi, lo, v))),
            pl.BlockSpec((1, d_ff, d), lambda v, f, vt, ve, vf, lo, hi: (ve[v], 0, 0),
                         pipeline_mode=pl.Buffered(1)),
            pl.BlockSpec((1, 1, d), lambda v, f, vt, ve, vf, lo, hi: (ve[v], 0, 0)),
        ],
        out_specs=pl.BlockSpec((tm, d), lambda v, f, vt, ve, vf, lo, hi: (vt[v], 0)),
        scratch_shapes=[pltpu.VMEM((nf, tm, tf), BF16)],
    )
    return pl.pallas_call(
        functools.partial(_experts_kernel, nf=nf, tf=tf),
        grid_spec=grid_spec,
        out_shape=jax.ShapeDtypeStruct((n_rows, d), F32),
        compiler_params=_params(2),
        name="experts",
    )(*plan, xs, w_gu, w_gu, b_gu, b_gu, w_d, b_d)


def _combine_kernel(pos_ref, nxt_ref, y_hbm, x1_ref, w_ref, fg_ref, o_ref, buf, sem, *, tm, n_steps):
    i = pl.program_id(0)
    slot = i % 2

    def fetch(p_ref, s):
        def body(t, carry):
            for kk in range(TOP_K):
                pltpu.make_async_copy(y_hbm.at[pl.ds(p_ref[0, 0, t * TOP_K + kk], 1)],
                                      buf.at[s, kk, pl.ds(t, 1)], sem.at[s]).start()
            return carry
        lax.fori_loop(0, tm, body, 0, unroll=8)

    @pl.when(i == 0)
    def _():
        fetch(pos_ref, 0)

    @pl.when(i + 1 < n_steps)
    def _():
        fetch(nxt_ref, 1 - slot)

    for kk in range(TOP_K):
        pltpu.make_async_copy(y_hbm.at[pl.ds(0, tm)], buf.at[slot, kk], sem.at[slot]).wait()
    acc = x1_ref[...]
    for kk in range(TOP_K):
        acc = acc + w_ref[:, kk:kk + 1] * buf[slot, kk]
    o_ref[...] = _rms(acc, fg_ref[...])


def _combine(pos, y_sorted, x1, wts, fg, *, row_start, rows):
    n, d = x1.shape
    tm = _pick(128, rows)
    n_steps = rows // tm
    t0 = row_start // tm
    pos3 = pos.reshape(n // tm, 1, tm * TOP_K)
    kernel = functools.partial(_combine_kernel, tm=tm, n_steps=n_steps)
    smem = lambda im: pl.BlockSpec((1, 1, tm * TOP_K), im, memory_space=pltpu.SMEM)
    return pl.pallas_call(
        kernel,
        grid=(n_steps,),
        in_specs=[
            smem(lambda i: (t0 + i, 0, 0)),
            smem(lambda i: (t0 + jnp.minimum(i + 1, n_steps - 1), 0, 0)),
            pl.BlockSpec(memory_space=pl.ANY),
            pl.BlockSpec((tm, d), lambda i: (t0 + i, 0)),
            pl.BlockSpec((tm, V7X_LANES), lambda i: (t0 + i, 0)),
            _resident((1, d)),
        ],
        out_specs=pl.BlockSpec((tm, d), lambda i: (i, 0)),
        out_shape=jax.ShapeDtypeStruct((rows, d), F32),
        scratch_shapes=[pltpu.VMEM((2, TOP_K, tm, d), F32), pltpu.SemaphoreType.DMA((2,))],
        compiler_params=_params(1),
        name="combine",
    )(pos3, pos3, y_sorted, x1, wts, fg)


def _rope_tables(n_tokens):
    rows = n_tokens // GRID_W
    axis_dim = HEAD_DIM // 2
    row = jnp.repeat(jnp.arange(rows, dtype=F32), GRID_W)
    col = jnp.tile(jnp.arange(GRID_W, dtype=F32), rows)
    inv_freq = ROPE_THETA ** (-jnp.arange(0, axis_dim, 2, dtype=F32) / axis_dim)
    ang_r = row[:, None] * inv_freq[None, :]
    ang_c = col[:, None] * inv_freq[None, :]
    ang = jnp.concatenate([ang_r, ang_r, ang_c, ang_c], axis=-1)
    quarter = jnp.arange(HEAD_DIM) % (HEAD_DIM // 2) < HEAD_DIM // 4
    return jnp.cos(ang), jnp.where(quarter[None, :], -1.0, 1.0).astype(F32) * jnp.sin(ang)


def kernel(x_prompt, x_sample, norm1_g, w_in, conv_w, conv_b, q_norm_g, k_norm_g, w_out, norm2_g,
           router_w, router_b, w_gate_up, b_gate_up, w_down, b_down, final_g):
    assert norm1_g.shape[0] == 1, "one layer"
    b_p, s_p, d = x_prompt.shape
    b_s, s_s, _ = x_sample.shape
    n_p, n_s = b_p * s_p, b_s * s_s
    n = n_p + n_s
    n_exp = router_w.shape[-1]
    assert n_exp <= V7X_LANES and n_p % s_s == 0
    kchunk = _pick(1024, math.gcd(s_p, s_s))
    xp = x_prompt.reshape(n_p, d)
    xs = x_sample.reshape(n_s, d)
    cos, sin_signed = _rope_tables(max(s_p, s_s))

    gb, u, qT, k, vT = _in_proj(
        xp, xs, norm1_g[0][None, :], w_in[0].astype(BF16), q_norm_g[0][None, :], k_norm_g[0][None, :],
        cos, sin_signed, s_p=s_p, s_s=s_s, kchunk=kchunk)
    wgu, wd = w_gate_up[0], w_down[0]
    attn_p, wgu_b = _attention(qT, k, vT, wgu.reshape(-1, wgu.shape[-1]), row_start=0, batch=b_p, seq=s_p,
                               kchunk=kchunk)
    attn_s, wd_b = _attention(qT, k, vT, wd.reshape(-1, wd.shape[-1]), row_start=n_p, batch=b_s, seq=s_s,
                              kchunk=kchunk)

    rw = jnp.zeros((d, V7X_LANES), BF16).at[:, :n_exp].set(router_w[0].astype(BF16))
    rb = jnp.full((1, V7X_LANES), NEG_PAD, F32).at[0, :n_exp].set(router_b[0])
    x1, h2, meta_i, meta_w, cnt = _out_proj(
        xp, xs, gb, u, attn_p, attn_s, conv_w[0], conv_b[0][None, :], w_out[0].astype(BF16),
        norm2_g[0][None, :], rw, rb, s_p=s_p, s_s=s_s)

    counts = cnt[0, :n_exp].astype(jnp.int32)
    starts = jnp.cumsum(counts) - counts
    ids = meta_i[:, :TOP_K]
    ranks = meta_i[:, TOP_K:2 * TOP_K]
    onehot = ids[:, :, None] == jnp.arange(n_exp, dtype=jnp.int32)[None, None, :]
    pos = ranks + jnp.sum(jnp.where(onehot, starts[None, None, :], 0), axis=-1)

    x_sorted = _dispatch(pos, h2)
    n_rows = n * TOP_K
    tm_e = _pick(512, n_rows)
    plan = _visit_plan(counts, n_rows, tm_e, n_rows // tm_e + n_exp - 1)
    y_sorted = _experts(plan, x_sorted, wgu_b.reshape(wgu.shape), b_gate_up[0][:, None, :],
                        wd_b.reshape(wd.shape), b_down[0][:, None, :])

    fg = final_g[None, :]
    y_p = _combine(pos, y_sorted, x1, meta_w, fg, row_start=0, rows=n_p)
    y_s = _combine(pos, y_sorted, x1, meta_w, fg, row_start=n_p, rows=n_s)
    return (y_p.reshape(b_p, s_p, d), y_s.reshape(b_s, s_s, d))
```

```python
import functools
import math

import jax
import jax.numpy as jnp
from jax import lax
from jax.experimental import pallas as pl
from jax.experimental.pallas import tpu as pltpu

HEAD_DIM = 128
N_KV_HEADS = 2
GRID_W = 64
ROPE_THETA = 10000.0
TOP_K = 4
SWIGLU_LIMIT = 7.0
SWIGLU_ALPHA = 1.702
NORM_EPS = 1e-6

V7X_LANES = 128
V7X_SUBLANES = 8
V7X_VMEM_BYTES = 64 * 1024 * 1024
VMEM_LIMIT = V7X_VMEM_BYTES - 8 * 1024 * 1024
NEG_PAD = -1e30

F32 = jnp.float32
BF16 = jnp.bfloat16
LOG2E = math.log2(math.e)


def _pick(target, n):
    t = min(target, n)
    while n % t:
        t //= 2
    return t


def _params(n_axes, vmem=VMEM_LIMIT):
    return pltpu.CompilerParams(dimension_semantics=("arbitrary",) * n_axes, vmem_limit_bytes=vmem)


def _resident(shape):
    nd = len(shape)
    return pl.BlockSpec(shape, lambda *_: (0,) * nd)


def _load_once(src_hbm, dst_vmem, sem):
    @pl.when(pl.program_id(0) == 0)
    def _():
        copy = pltpu.make_async_copy(src_hbm, dst_vmem, sem)
        copy.start()
        copy.wait()


def _rms(x, g):
    ms = jnp.mean(x * x, axis=-1, keepdims=True)
    return x * lax.rsqrt(ms + NORM_EPS) * g


def _in_proj_kernel(xp_ref, xs_ref, g1_ref, w_hbm, qg_ref, kg_ref, cos_ref, sin_ref,
                    gb_ref, u_ref, qT_ref, k_ref, vT_ref, w_ref, w_sem, *, n_prompt_tiles, conv_w, n_q, q_scale):
    _load_once(w_hbm, w_ref, w_sem)
    i = pl.program_id(0)
    x = jnp.where(i < n_prompt_tiles, xp_ref[...], xs_ref[...])
    h = _rms(x, g1_ref[...]).astype(BF16)

    def proj(c0, c1):
        return jnp.dot(h, w_ref[:, c0:c1], preferred_element_type=F32)

    gb_ref[...] = proj(0, conv_w)
    u_ref[...] = proj(conv_w, 2 * conv_w) * proj(2 * conv_w, 3 * conv_w)

    cos = cos_ref[...]
    sin = sin_ref[...]
    lane = lax.broadcasted_iota(jnp.int32, (1, HEAD_DIM), 1)
    first_half = (lane % (HEAD_DIM // 2)) < (HEAD_DIM // 4)

    def norm_rope(xh, g):
        y = _rms(xh, g)
        up = pltpu.roll(y, HEAD_DIM - HEAD_DIM // 4, 1)
        dn = pltpu.roll(y, HEAD_DIM // 4, 1)
        return y * cos + jnp.where(first_half, up, dn) * sin

    q0 = 3 * conv_w
    q = proj(q0, q0 + n_q * HEAD_DIM)
    for hd in range(n_q):
        qh = norm_rope(q[:, hd * HEAD_DIM:(hd + 1) * HEAD_DIM], qg_ref[...]) * q_scale
        qT_ref[hd] = qh.T.astype(BF16)
    k0 = q0 + n_q * HEAD_DIM
    kv = proj(k0, k0 + 2 * N_KV_HEADS * HEAD_DIM)
    for hd in range(N_KV_HEADS):
        kh = norm_rope(kv[:, hd * HEAD_DIM:(hd + 1) * HEAD_DIM], kg_ref[...])
        k_ref[:, hd * HEAD_DIM:(hd + 1) * HEAD_DIM] = kh.astype(BF16)
        vh = kv[:, (N_KV_HEADS + hd) * HEAD_DIM:(N_KV_HEADS + hd + 1) * HEAD_DIM]
        vT_ref[hd, 0] = vh.T.astype(BF16)


def _in_proj(xp, xs, g1, w_in, qg, kg, cos, sin_signed, *, s_p, s_s, kchunk):
    n_p, d = xp.shape
    n_s = xs.shape[0]
    n = n_p + n_s
    conv_w = d // 2
    n_q = (d - conv_w) // HEAD_DIM
    tm = _pick(256, math.gcd(s_p, s_s))
    npt, spt, sst = n_p // tm, s_p // tm, s_s // tm
    r = kchunk // tm
    q_scale = LOG2E / math.sqrt(HEAD_DIM)

    def pos_map(i):
        return (jnp.where(i < npt, i % spt, (i - npt) % sst), 0)

    kernel = functools.partial(_in_proj_kernel, n_prompt_tiles=npt, conv_w=conv_w, n_q=n_q, q_scale=q_scale)
    return pl.pallas_call(
        kernel,
        grid=(n // tm,),
        in_specs=[
            pl.BlockSpec((tm, d), lambda i: (jnp.minimum(i, npt - 1), 0)),
            pl.BlockSpec((tm, d), lambda i: (jnp.maximum(i - npt, 0), 0)),
            _resident((1, d)),
            pl.BlockSpec(memory_space=pl.ANY),
            _resident((1, HEAD_DIM)),
            _resident((1, HEAD_DIM)),
            pl.BlockSpec((tm, HEAD_DIM), pos_map),
            pl.BlockSpec((tm, HEAD_DIM), pos_map),
        ],
        out_specs=[
            pl.BlockSpec((tm, conv_w), lambda i: (i, 0)),
            pl.BlockSpec((tm, conv_w), lambda i: (i, 0)),
            pl.BlockSpec((n_q, HEAD_DIM, tm), lambda i: (0, 0, i)),
            pl.BlockSpec((tm, N_KV_HEADS * HEAD_DIM), lambda i: (i, 0)),
            pl.BlockSpec((N_KV_HEADS, 1, HEAD_DIM, tm), lambda i: (0, i // r, 0, i % r)),
        ],
        out_shape=[
            jax.ShapeDtypeStruct((n, conv_w), F32),
            jax.ShapeDtypeStruct((n, conv_w), F32),
            jax.ShapeDtypeStruct((n_q, HEAD_DIM, n), BF16),
            jax.ShapeDtypeStruct((n, N_KV_HEADS * HEAD_DIM), BF16),
            jax.ShapeDtypeStruct((N_KV_HEADS, n // kchunk, HEAD_DIM, kchunk), BF16),
        ],
        scratch_shapes=[pltpu.VMEM(w_in.shape, w_in.dtype), pltpu.SemaphoreType.DMA(())],
        compiler_params=_params(1),
        name="in_proj",
    )(xp, xs, g1, w_in, qg, kg, cos, sin_signed)


def _attention_kernel(qT_ref, k_ref, vT_ref, w_ref, o_ref, wb_ref, m_ref, l_ref, acc_ref, s_ref, mx_ref,
                      *, n_chunks, kchunk, ksub, group):
    wb_ref[...] = w_ref[...].astype(BF16)

    m_ref[...] = jnp.full(m_ref.shape, -jnp.inf, F32)
    l_ref[...] = jnp.zeros(l_ref.shape, F32)
    acc_ref[...] = jnp.zeros(acc_ref.shape, F32)
    sub = V7X_SUBLANES

    n_sub = kchunk // ksub

    def scores_sub(c, g, par, j, mx):
        r0 = pl.multiple_of(c * kchunk + j * ksub, ksub)
        sT = jnp.dot(k_ref[pl.ds(r0, ksub), :], qT_ref[g], preferred_element_type=F32)
        s_ref[par, j * ksub:(j + 1) * ksub, :] = sT
        cm = jnp.max(sT.reshape(ksub // sub, sub, sT.shape[1]), axis=0)
        return cm if mx is None else jnp.maximum(mx, cm)

    def consume_sub(c, par, j, m_new, pv, l):
        p = jnp.exp2(s_ref[par, j * ksub:(j + 1) * ksub, :] - m_new)
        d = jnp.dot(vT_ref[0, c, :, j * ksub:(j + 1) * ksub], p.astype(BF16), preferred_element_type=F32)
        ps = jnp.sum(p.reshape(ksub // sub, sub, p.shape[1]), axis=0)
        return (d if pv is None else pv + d), (ps if l is None else l + ps)

    def pair(c_next, g_next, par_next, c, g, par):
        m_old = m_ref[g]
        m_new = jnp.maximum(m_old, jnp.max(mx_ref[par], axis=0, keepdims=True))
        alpha = jnp.exp2(m_old - m_new)
        mx = pv = l = None
        for j in range(n_sub):
            mx = scores_sub(c_next, g_next, par_next, j, mx)
            pv, l = consume_sub(c, par, j, m_new, pv, l)
        mx_ref[par_next] = mx
        l_ref[g] = alpha * l_ref[g] + jnp.sum(l, axis=0, keepdims=True)
        acc_ref[g] = alpha * acc_ref[g] + pv
        m_ref[g] = m_new

    def parity(c, g):
        return (g % 2) if group % 2 == 0 else (c * group + g) % 2

    mx0 = None
    for j in range(n_sub):
        mx0 = scores_sub(0, 0, 0, j, mx0)
    mx_ref[0] = mx0

    def chunk(c, carry):
        for g in range(group):
            if g + 1 < group:
                pair(c, g + 1, parity(c, g + 1), c, g, parity(c, g))
            else:
                pair(jnp.minimum(c + 1, n_chunks - 1), 0, parity(c + 1, 0), c, g, parity(c, g))
        return carry

    lax.fori_loop(0, n_chunks, chunk, 0, unroll=math.gcd(n_chunks, 4))
    for g in range(group):
        o = acc_ref[g] / l_ref[g]
        o_ref[:, g * HEAD_DIM:(g + 1) * HEAD_DIM] = o.T.astype(o_ref.dtype)


def _attention(qT, k, vT, w, *, row_start, batch, seq, kchunk):
    n_q = qT.shape[0]
    group = n_q // N_KV_HEADS
    tq = _pick(256, seq)
    nqt = seq // tq
    q0 = row_start // tq
    s0 = row_start // seq
    n_chunks = seq // kchunk
    n_steps = batch * N_KV_HEADS * nqt
    w_rows, w_cols = w.shape
    wr = w_rows // n_steps
    assert wr * n_steps == w_rows and wr % (2 * V7X_SUBLANES) == 0, "weight rows must split evenly over the grid"
    step = lambda b, h, i: ((b * N_KV_HEADS + h) * nqt + i, 0)
    kernel = functools.partial(_attention_kernel, n_chunks=n_chunks, kchunk=kchunk, ksub=_pick(256, kchunk),
                               group=group)
    return pl.pallas_call(
        kernel,
        grid=(batch, N_KV_HEADS, nqt),
        in_specs=[
            pl.BlockSpec((group, HEAD_DIM, tq), lambda b, h, i: (h, 0, q0 + b * nqt + i)),
            pl.BlockSpec((seq, HEAD_DIM), lambda b, h, i: (s0 + b, h)),
            pl.BlockSpec((1, n_chunks, HEAD_DIM, kchunk), lambda b, h, i: (h, s0 + b, 0, 0)),
            pl.BlockSpec((wr, w_cols), step),
        ],
        out_specs=[
            pl.BlockSpec((tq, group * HEAD_DIM), lambda b, h, i: (b * nqt + i, h)),
            pl.BlockSpec((wr, w_cols), step),
        ],
        out_shape=[
            jax.ShapeDtypeStruct((batch * seq, n_q * HEAD_DIM), BF16),
            jax.ShapeDtypeStruct((w_rows, w_cols), BF16),
        ],
        scratch_shapes=[
            pltpu.VMEM((group, 1, tq), F32),
            pltpu.VMEM((group, 1, tq), F32),
            pltpu.VMEM((group, HEAD_DIM, tq), F32),
            pltpu.VMEM((2, kchunk, tq), F32),
            pltpu.VMEM((2, V7X_SUBLANES, tq), F32),
        ],
        compiler_params=_params(3),
        name="attention",
    )(qT, k, vT, w)


def _out_proj_kernel(xp_ref, xs_ref, gb_ref, u_ref, up_ref, un_ref, ap_ref, as_ref, cw_ref, cb_ref, wo_hbm,
                     g2_ref, rw_ref, rb_ref,
                     x1_ref, h2_ref, mi_ref, mw_ref, cnt_ref, ubuf, wo_ref, wo_sem,
                     *, n_prompt_tiles, spt, sst, conv_w, tm):
    _load_once(wo_hbm, wo_ref, wo_sem)
    i = pl.program_id(0)
    is_p = i < n_prompt_tiles
    local = jnp.where(is_p, i % spt, (i - n_prompt_tiles) % sst)
    per_seq = jnp.where(is_p, spt, sst)
    keep_prev = (local > 0).astype(F32)
    keep_next = (local < per_seq - 1).astype(F32)

    sub = V7X_SUBLANES
    ubuf[0:sub, :] = up_ref[...] * keep_prev
    ubuf[sub:sub + tm, :] = u_ref[...]
    ubuf[sub + tm:2 * sub + tm, :] = un_ref[...] * keep_next
    conv = (ubuf[sub - 1:sub - 1 + tm, :] * cw_ref[0:1, :] + ubuf[sub:sub + tm, :] * cw_ref[1:2, :]
            + ubuf[sub + 1:sub + 1 + tm, :] * cw_ref[2:3, :] + cb_ref[...])
    conv_out = (gb_ref[...] * conv).astype(BF16)
    attn = jnp.where(is_p, ap_ref[...], as_ref[...])
    x = jnp.where(is_p, xp_ref[...], xs_ref[...])
    x1 = (x + jnp.dot(conv_out, wo_ref[0:conv_w, :], preferred_element_type=F32)
          + jnp.dot(attn, wo_ref[conv_w:, :], preferred_element_type=F32))
    x1_ref[...] = x1
    h2 = _rms(x1, g2_ref[...])
    h2_ref[...] = h2

    logits = jnp.dot(h2.astype(BF16), rw_ref[...], preferred_element_type=F32) + rb_ref[...]
    lane = lax.broadcasted_iota(jnp.int32, logits.shape, 1)
    vals, hots = [], []
    work = logits
    for _ in range(TOP_K):
        m = jnp.max(work, axis=-1, keepdims=True)
        idx = jnp.min(jnp.where(work == m, lane, V7X_LANES), axis=-1, keepdims=True)
        hot = lane == idx
        vals.append(m)
        hots.append((hot, idx))
        work = jnp.where(hot, -jnp.inf, work)
    es = [jnp.exp(v - vals[0]) for v in vals]
    den = es[0] + es[1] + es[2] + es[3]

    @pl.when(i == 0)
    def _():
        cnt_ref[...] = jnp.zeros(cnt_ref.shape, F32)

    sel = jnp.zeros(logits.shape, F32)
    for hot, _ in hots:
        sel = sel + hot.astype(F32)
    row = lax.broadcasted_iota(jnp.int32, (tm, tm), 0)
    col = lax.broadcasted_iota(jnp.int32, (tm, tm), 1)
    lower = jnp.where(col < row, 1.0, 0.0).astype(BF16)
    before = jnp.dot(lower, sel.astype(BF16), preferred_element_type=F32) + cnt_ref[0:1, :]
    mi = jnp.zeros(logits.shape, jnp.int32)
    mw = jnp.zeros(logits.shape, F32)
    for kk, (hot, idx) in enumerate(hots):
        rank = jnp.sum(jnp.where(hot, before, 0.0), axis=-1, keepdims=True).astype(jnp.int32)
        mi = jnp.where(lane == kk, idx, mi)
        mi = jnp.where(lane == TOP_K + kk, rank, mi)
        mw = jnp.where(lane == kk, es[kk] / den, mw)
    mi_ref[...] = mi
    mw_ref[...] = mw
    cnt_ref[...] = cnt_ref[...] + jnp.sum(sel, axis=0, keepdims=True)


def _out_proj(xp, xs, gb, u, attn_p, attn_s, conv_w_, conv_b, w_out, g2, rw, rb, *, s_p, s_s):
    n_p, d = xp.shape
    n_s = xs.shape[0]
    n = n_p + n_s
    conv_w = gb.shape[1]
    a_w = attn_p.shape[1]
    tm = _pick(256, math.gcd(s_p, s_s))
    npt, spt, sst = n_p // tm, s_p // tm, s_s // tm
    sub = V7X_SUBLANES
    hb = tm // sub
    kernel = functools.partial(_out_proj_kernel, n_prompt_tiles=npt, spt=spt, sst=sst, conv_w=conv_w, tm=tm)
    row = lambda i: (i, 0)
    return pl.pallas_call(
        kernel,
        grid=(n // tm,),
        in_specs=[
            pl.BlockSpec((tm, d), lambda i: (jnp.minimum(i, npt - 1), 0)),
            pl.BlockSpec((tm, d), lambda i: (jnp.maximum(i - npt, 0), 0)),
            pl.BlockSpec((tm, conv_w), row),
            pl.BlockSpec((tm, conv_w), row),
            pl.BlockSpec((sub, conv_w), lambda i: (jnp.maximum(i * hb - 1, 0), 0)),
            pl.BlockSpec((sub, conv_w), lambda i: (jnp.minimum((i + 1) * hb, n // sub - 1), 0)),
            pl.BlockSpec((tm, a_w), lambda i: (jnp.minimum(i, npt - 1), 0)),
            pl.BlockSpec((tm, a_w), lambda i: (jnp.maximum(i - npt, 0), 0)),
            _resident(conv_w_.shape),
            _resident((1, conv_w)),
            pl.BlockSpec(memory_space=pl.ANY),
            _resident((1, d)),
            _resident(rw.shape),
            _resident(rb.shape),
        ],
        out_specs=[
            pl.BlockSpec((tm, d), row),
            pl.BlockSpec((tm, d), row),
            pl.BlockSpec((tm, V7X_LANES), row),
            pl.BlockSpec((tm, V7X_LANES), row),
            pl.BlockSpec((V7X_SUBLANES, V7X_LANES), lambda i: (0, 0)),
        ],
        out_shape=[
            jax.ShapeDtypeStruct((n, d), F32),
            jax.ShapeDtypeStruct((n, d), F32),
            jax.ShapeDtypeStruct((n, V7X_LANES), jnp.int32),
            jax.ShapeDtypeStruct((n, V7X_LANES), F32),
            jax.ShapeDtypeStruct((V7X_SUBLANES, V7X_LANES), F32),
        ],
        scratch_shapes=[pltpu.VMEM((tm + 2 * sub, conv_w), F32), pltpu.VMEM(w_out.shape, w_out.dtype),
                        pltpu.SemaphoreType.DMA(())],
        compiler_params=_params(1),
        name="out_proj",
    )(xp, xs, gb, u, u, u, attn_p, attn_s, conv_w_, conv_b, w_out, g2, rw, rb)


def _dispatch_kernel(pos_ref, h_hbm, xs_hbm, buf, sem_in, sem_out, *, tm, n_tiles):
    i = pl.program_id(0)
    slot = i % 2

    def load(tile, s):
        return pltpu.make_async_copy(h_hbm.at[pl.ds(tile * tm, tm)], buf.at[s], sem_in.at[s])

    def row_out(s, t, dst_row):
        return pltpu.make_async_copy(buf.at[s, pl.ds(t, 1)], xs_hbm.at[pl.ds(dst_row, 1)], sem_out.at[s])

    def drain(s):
        for _ in range(TOP_K):
            pltpu.make_async_copy(buf.at[s], xs_hbm.at[pl.ds(0, tm)], sem_out.at[s]).wait()

    @pl.when(i == 0)
    def _():
        load(0, 0).start()

    @pl.when(i > 0)
    def _():
        drain(1 - slot)

    @pl.when(i + 1 < n_tiles)
    def _():
        load(i + 1, 1 - slot).start()

    load(i, slot).wait()

    def issue(t, carry):
        for kk in range(TOP_K):
            row_out(slot, t, pos_ref[0, 0, t * TOP_K + kk]).start()
        return carry

    lax.fori_loop(0, tm, issue, 0, unroll=8)

    @pl.when(i == n_tiles - 1)
    def _():
        drain(slot)


def _dispatch(pos, h2):
    n, d = h2.shape
    tm = _pick(256, n)
    n_tiles = n // tm
    pos3 = pos.reshape(n_tiles, 1, tm * TOP_K)
    kernel = functools.partial(_dispatch_kernel, tm=tm, n_tiles=n_tiles)
    return pl.pallas_call(
        kernel,
        grid=(n_tiles,),
        in_specs=[
            pl.BlockSpec((1, 1, tm * TOP_K), lambda i: (i, 0, 0), memory_space=pltpu.SMEM),
            pl.BlockSpec(memory_space=pl.ANY),
        ],
        out_specs=pl.BlockSpec(memory_space=pl.ANY),
        out_shape=jax.ShapeDtypeStruct((n * TOP_K, d), h2.dtype),
        scratch_shapes=[pltpu.VMEM((2, tm, d), h2.dtype), pltpu.SemaphoreType.DMA((2,)),
                        pltpu.SemaphoreType.DMA((2,))],
        compiler_params=_params(1),
        name="dispatch",
    )(pos3, h2)


def _experts_kernel(vt_ref, ve_ref, vfirst_ref, vlo_ref, vhi_ref,
                    x_ref, wg_ref, wu_ref, bg_ref, bu_ref, wd_ref, bd_ref, o_ref, act_ref, *, nf, tf):
    v = pl.program_id(0)
    f = pl.program_id(1)
    lo = vlo_ref[v]
    hi = vhi_ref[v]

    @pl.when(hi > lo)
    def _():
        xb = x_ref[...].astype(BF16)
        gate = jnp.dot(xb, wg_ref[0], preferred_element_type=F32) + bg_ref[0]
        up = jnp.dot(xb, wu_ref[0], preferred_element_type=F32) + bu_ref[0]
        gate = jnp.minimum(gate, SWIGLU_LIMIT)
        up = jnp.clip(up, -SWIGLU_LIMIT, SWIGLU_LIMIT)
        glu = gate * jax.nn.sigmoid(gate * SWIGLU_ALPHA)
        rows = lax.broadcasted_iota(jnp.int32, (xb.shape[0], 1), 0)
        mine = (rows >= lo) & (rows < hi)
        act_ref[f] = jnp.where(mine, (up + 1.0) * glu, 0.0).astype(BF16)

        def down():
            y = jnp.dot(act_ref[0], wd_ref[0, 0:tf, :], preferred_element_type=F32)
            for j in range(1, nf):
                y = y + jnp.dot(act_ref[j], wd_ref[0, j * tf:(j + 1) * tf, :], preferred_element_type=F32)
            return y + jnp.where(mine, bd_ref[0], 0.0)

        last = f == nf - 1
        fresh = vfirst_ref[v] == 1

        @pl.when(jnp.logical_and(last, fresh))
        def _():
            o_ref[...] = down()

        @pl.when(jnp.logical_and(last, jnp.logical_not(fresh)))
        def _():
            o_ref[...] = o_ref[...] + down()


def _visit_plan(counts, n_rows, tm, n_visits):
    n_exp = counts.shape[0]
    ends = jnp.cumsum(counts)
    starts = ends - counts
    first_tile = starts // tm
    last_tile = jnp.maximum(ends - 1, 0) // tm
    nvis = jnp.where(counts > 0, last_tile - first_tile + 1, 0)
    vis_end = jnp.cumsum(nvis)
    vis_start = vis_end - nvis
    total = vis_end[-1]
    v = jnp.arange(n_visits, dtype=jnp.int32)
    e = jnp.sum((v[:, None] >= vis_end[None, :]).astype(jnp.int32), axis=1)
    e = jnp.minimum(e, n_exp - 1)
    valid = v < total
    onehot = (e[:, None] == jnp.arange(n_exp, dtype=jnp.int32)[None, :]).astype(jnp.int32)
    pickv = lambda a: jnp.sum(onehot * a[None, :], axis=1)
    tile = pickv(first_tile) + (v - pickv(vis_start))
    lo = jnp.maximum(pickv(starts), tile * tm) - tile * tm
    hi = jnp.minimum(pickv(ends), (tile + 1) * tm) - tile * tm
    last_e = jnp.sum((total - 1 >= vis_end).astype(jnp.int32))
    tile = jnp.where(valid, tile, n_rows // tm - 1)
    e = jnp.where(valid, e, jnp.minimum(last_e, n_exp - 1))
    lo = jnp.where(valid, lo, 0)
    hi = jnp.where(valid, hi, 0)
    first = jnp.where(valid & (lo == 0), 1, 0)
    i32 = lambda a: a.astype(jnp.int32)
    return i32(tile), i32(e), i32(first), i32(lo), i32(hi)


def _experts(plan, xs, w_gu, b_gu, w_d, b_d):
    n_rows, d = xs.shape
    n_exp, _, two_f = w_gu.shape
    d_ff = two_f // 2
    tm = _pick(512, n_rows)
    tf = _pick(1024, d_ff)
    nf = d_ff // tf
    n_visits = plan[0].shape[0]

    def fx(f, hi, lo, v):
        return jnp.where(hi[v] > lo[v], f, 0)

    grid_spec = pltpu.PrefetchScalarGridSpec(
        num_scalar_prefetch=5,
        grid=(n_visits, nf),
        in_specs=[
            pl.BlockSpec((tm, d), lambda v, f, vt, ve, vf, lo, hi: (vt[v], 0)),
            pl.BlockSpec((1, d, tf), lambda v, f, vt, ve, vf, lo, hi: (ve[v], 0, fx(f, hi, lo, v))),
            pl.BlockSpec((1, d, tf), lambda v, f, vt, ve, vf, lo, hi: (ve[v], 0, nf + fx(f, hi, lo, v))),
            pl.BlockSpec((1, 1, tf), lambda v, f, vt, ve, vf, lo, hi: (ve[v], 0, fx(f, hi, lo, v))),
            pl.BlockSpec((1, 1, tf), lambda v, f, vt, ve, vf, lo, hi: (ve[v], 0, nf + fx(f, hi, lo, v))),
            pl.BlockSpec((1, d_ff, d), lambda v, f, vt, ve, vf, lo, hi: (ve[v], 0, 0)),
            pl.BlockSpec((1, 1, d), lambda v, f, vt, ve, vf, lo, hi: (ve[v], 0, 0)),
        ],
        out_specs=pl.BlockSpec((tm, d), lambda v, f, vt, ve, vf, lo, hi: (vt[v], 0)),
        scratch_shapes=[pltpu.VMEM((nf, tm, tf), BF16)],
    )
    return pl.pallas_call(
        functools.partial(_experts_kernel, nf=nf, tf=tf),
        grid_spec=grid_spec,
        out_shape=jax.ShapeDtypeStruct((n_rows, d), F32),
        compiler_params=_params(2, vmem=V7X_VMEM_BYTES - 4 * 1024 * 1024),
        name="experts",
    )(*plan, xs, w_gu, w_gu, b_gu, b_gu, w_d, b_d)


def _combine_kernel(pos_ref, nxt_ref, y_hbm, x1_ref, w_ref, fg_ref, o_ref, buf, sem, *, tm, n_steps):
    i = pl.program_id(0)
    slot = i % 2

    def fetch(p_ref, s):
        def body(t, carry):
            for kk in range(TOP_K):
                pltpu.make_async_copy(y_hbm.at[pl.ds(p_ref[0, 0, t * TOP_K + kk], 1)],
                                      buf.at[s, kk, pl.ds(t, 1)], sem.at[s]).start()
            return carry
        lax.fori_loop(0, tm, body, 0, unroll=8)

    @pl.when(i == 0)
    def _():
        fetch(pos_ref, 0)

    @pl.when(i + 1 < n_steps)
    def _():
        fetch(nxt_ref, 1 - slot)

    for kk in range(TOP_K):
        pltpu.make_async_copy(y_hbm.at[pl.ds(0, tm)], buf.at[slot, kk], sem.at[slot]).wait()
    acc = x1_ref[...]
    for kk in range(TOP_K):
        acc = acc + w_ref[:, kk:kk + 1] * buf[slot, kk]
    o_ref[...] = _rms(acc, fg_ref[...])


def _combine(pos, y_sorted, x1, wts, fg, *, row_start, rows):
    n, d = x1.shape
    tm = _pick(128, rows)
    n_steps = rows // tm
    t0 = row_start // tm
    pos3 = pos.reshape(n // tm, 1, tm * TOP_K)
    kernel = functools.partial(_combine_kernel, tm=tm, n_steps=n_steps)
    smem = lambda im: pl.BlockSpec((1, 1, tm * TOP_K), im, memory_space=pltpu.SMEM)
    return pl.pallas_call(
        kernel,
        grid=(n_steps,),
        in_specs=[
            smem(lambda i: (t0 + i, 0, 0)),
            smem(lambda i: (t0 + jnp.minimum(i + 1, n_steps - 1), 0, 0)),
            pl.BlockSpec(memory_space=pl.ANY),
            pl.BlockSpec((tm, d), lambda i: (t0 + i, 0)),
            pl.BlockSpec((tm, V7X_LANES), lambda i: (t0 + i, 0)),
            _resident((1, d)),
        ],
        out_specs=pl.BlockSpec((tm, d), lambda i: (i, 0)),
        out_shape=jax.ShapeDtypeStruct((rows, d), F32),
        scratch_shapes=[pltpu.VMEM((2, TOP_K, tm, d), F32), pltpu.SemaphoreType.DMA((2,))],
        compiler_params=_params(1),
        name="combine",
    )(pos3, pos3, y_sorted, x1, wts, fg)


def _rope_tables(n_tokens):
    rows = n_tokens // GRID_W
    axis_dim = HEAD_DIM // 2
    row = jnp.repeat(jnp.arange(rows, dtype=F32), GRID_W)
    col = jnp.tile(jnp.arange(GRID_W, dtype=F32), rows)
    inv_freq = ROPE_THETA ** (-jnp.arange(0, axis_dim, 2, dtype=F32) / axis_dim)
    ang_r = row[:, None] * inv_freq[None, :]
    ang_c = col[:, None] * inv_freq[None, :]
    ang = jnp.concatenate([ang_r, ang_r, ang_c, ang_c], axis=-1)
    quarter = jnp.arange(HEAD_DIM) % (HEAD_DIM // 2) < HEAD_DIM // 4
    return jnp.cos(ang), jnp.where(quarter[None, :], -1.0, 1.0).astype(F32) * jnp.sin(ang)


def kernel(x_prompt, x_sample, norm1_g, w_in, conv_w, conv_b, q_norm_g, k_norm_g, w_out, norm2_g,
           router_w, router_b, w_gate_up, b_gate_up, w_down, b_down, final_g):
    assert norm1_g.shape[0] == 1, "one layer"
    b_p, s_p, d = x_prompt.shape
    b_s, s_s, _ = x_sample.shape
    n_p, n_s = b_p * s_p, b_s * s_s
    n = n_p + n_s
    n_exp = router_w.shape[-1]
    assert n_exp <= V7X_LANES and n_p % s_s == 0
    kchunk = _pick(1024, math.gcd(s_p, s_s))
    xp = x_prompt.reshape(n_p, d)
    xs = x_sample.reshape(n_s, d)
    cos, sin_signed = _rope_tables(max(s_p, s_s))

    gb, u, qT, k, vT = _in_proj(
        xp, xs, norm1_g[0][None, :], w_in[0].astype(BF16), q_norm_g[0][None, :], k_norm_g[0][None, :],
        cos, sin_signed, s_p=s_p, s_s=s_s, kchunk=kchunk)
    wgu, wd = w_gate_up[0], w_down[0]
    attn_p, wgu_b = _attention(qT, k, vT, wgu.reshape(-1, wgu.shape[-1]), row_start=0, batch=b_p, seq=s_p,
                               kchunk=kchunk)
    attn_s, wd_b = _attention(qT, k, vT, wd.reshape(-1, wd.shape[-1]), row_start=n_p, batch=b_s, seq=s_s,
                              kchunk=kchunk)

    rw = jnp.zeros((d, V7X_LANES), BF16).at[:, :n_exp].set(router_w[0].astype(BF16))
    rb = jnp.full((1, V7X_LANES), NEG_PAD, F32).at[0, :n_exp].set(router_b[0])
    x1, h2, meta_i, meta_w, cnt = _out_proj(
        xp, xs, gb, u, attn_p, attn_s, conv_w[0], conv_b[0][None, :], w_out[0].astype(BF16),
        norm2_g[0][None, :], rw, rb, s_p=s_p, s_s=s_s)

    counts = cnt[0, :n_exp].astype(jnp.int32)
    starts = jnp.cumsum(counts) - counts
    ids = meta_i[:, :TOP_K]
    ranks = meta_i[:, TOP_K:2 * TOP_K]
    onehot = ids[:, :, None] == jnp.arange(n_exp, dtype=jnp.int32)[None, None, :]
    pos = ranks + jnp.sum(jnp.where(onehot, starts[None, None, :], 0), axis=-1)

    x_sorted = _dispatch(pos, h2)
    n_rows = n * TOP_K
    tm_e = _pick(512, n_rows)
    plan = _visit_plan(counts, n_rows, tm_e, n_rows // tm_e + n_exp - 1)
    y_sorted = _experts(plan, x_sorted, wgu_b.reshape(wgu.shape), b_gate_up[0][:, None, :],
                        wd_b.reshape(wd.shape), b_down[0][:, None, :])

    fg = final_g[None, :]
    y_p = _combine(pos, y_sorted, x1, meta_w, fg, row_start=0, rows=n_p)
    y_s = _combine(pos, y_sorted, x1, meta_w, fg, row_start=n_p, rows=n_s)
    return (y_p.reshape(b_p, s_p, d), y_s.reshape(b_s, s_s, d))
```

```python
import functools
import math

import jax
import jax.numpy as jnp
from jax import lax
from jax.experimental import pallas as pl
from jax.experimental.pallas import tpu as pltpu

HEAD_DIM = 128
N_KV_HEADS = 2
GRID_W = 64
ROPE_THETA = 10000.0
TOP_K = 4
SWIGLU_LIMIT = 7.0
SWIGLU_ALPHA = 1.702
NORM_EPS = 1e-6

V7X_LANES = 128
V7X_SUBLANES = 8
V7X_VMEM_BYTES = 64 * 1024 * 1024
VMEM_LIMIT = V7X_VMEM_BYTES - 8 * 1024 * 1024
NEG_PAD = -1e30

F32 = jnp.float32
BF16 = jnp.bfloat16
LOG2E = math.log2(math.e)


def _pick(target, n):
    t = min(target, n)
    while n % t:
        t //= 2
    return t


def _params(n_axes, vmem=VMEM_LIMIT):
    return pltpu.CompilerParams(dimension_semantics=("arbitrary",) * n_axes, vmem_limit_bytes=vmem)


def _resident(shape):
    nd = len(shape)
    return pl.BlockSpec(shape, lambda *_: (0,) * nd)


def _load_once(src_hbm, dst_vmem, sem):
    @pl.when(pl.program_id(0) == 0)
    def _():
        copy = pltpu.make_async_copy(src_hbm, dst_vmem, sem)
        copy.start()
        copy.wait()


def _rms(x, g):
    ms = jnp.mean(x * x, axis=-1, keepdims=True)
    return x * lax.rsqrt(ms + NORM_EPS) * g


def _in_proj_kernel(xp_ref, xs_ref, g1_ref, w_hbm, qg_ref, kg_ref, cos_ref, sin_ref,
                    gb_ref, u_ref, qT_ref, k_ref, vT_ref, w_ref, w_sem, *, n_prompt_tiles, conv_w, n_q, q_scale):
    _load_once(w_hbm, w_ref, w_sem)
    i = pl.program_id(0)
    x = jnp.where(i < n_prompt_tiles, xp_ref[...], xs_ref[...])
    h = _rms(x, g1_ref[...]).astype(BF16)

    def proj(c0, c1):
        return jnp.dot(h, w_ref[:, c0:c1], preferred_element_type=F32)

    gb_ref[...] = proj(0, conv_w)
    u_ref[...] = proj(conv_w, 2 * conv_w) * proj(2 * conv_w, 3 * conv_w)

    cos = cos_ref[...]
    sin = sin_ref[...]
    lane = lax.broadcasted_iota(jnp.int32, (1, HEAD_DIM), 1)
    first_half = (lane % (HEAD_DIM // 2)) < (HEAD_DIM // 4)

    def norm_rope(xh, g):
        y = _rms(xh, g)
        up = pltpu.roll(y, HEAD_DIM - HEAD_DIM // 4, 1)
        dn = pltpu.roll(y, HEAD_DIM // 4, 1)
        return y * cos + jnp.where(first_half, up, dn) * sin

    q0 = 3 * conv_w
    q = proj(q0, q0 + n_q * HEAD_DIM)
    for hd in range(n_q):
        qh = norm_rope(q[:, hd * HEAD_DIM:(hd + 1) * HEAD_DIM], qg_ref[...]) * q_scale
        qT_ref[hd] = qh.T.astype(BF16)
    k0 = q0 + n_q * HEAD_DIM
    kv = proj(k0, k0 + 2 * N_KV_HEADS * HEAD_DIM)
    for hd in range(N_KV_HEADS):
        kh = norm_rope(kv[:, hd * HEAD_DIM:(hd + 1) * HEAD_DIM], kg_ref[...])
        k_ref[:, hd * HEAD_DIM:(hd + 1) * HEAD_DIM] = kh.astype(BF16)
        vh = kv[:, (N_KV_HEADS + hd) * HEAD_DIM:(N_KV_HEADS + hd + 1) * HEAD_DIM]
        vT_ref[hd, 0] = vh.T.astype(BF16)


def _in_proj(xp, xs, g1, w_in, qg, kg, cos, sin_signed, *, s_p, s_s, kchunk):
    n_p, d = xp.shape
    n_s = xs.shape[0]
    n = n_p + n_s
    conv_w = d // 2
    n_q = (d - conv_w) // HEAD_DIM
    tm = _pick(256, math.gcd(s_p, s_s))
    npt, spt, sst = n_p // tm, s_p // tm, s_s // tm
    r = kchunk // tm
    q_scale = LOG2E / math.sqrt(HEAD_DIM)

    def pos_map(i):
        return (jnp.where(i < npt, i % spt, (i - npt) % sst), 0)

    kernel = functools.partial(_in_proj_kernel, n_prompt_tiles=npt, conv_w=conv_w, n_q=n_q, q_scale=q_scale)
    return pl.pallas_call(
        kernel,
        grid=(n // tm,),
        in_specs=[
            pl.BlockSpec((tm, d), lambda i: (jnp.minimum(i, npt - 1), 0)),
            pl.BlockSpec((tm, d), lambda i: (jnp.maximum(i - npt, 0), 0)),
            _resident((1, d)),
            pl.BlockSpec(memory_space=pl.ANY),
            _resident((1, HEAD_DIM)),
            _resident((1, HEAD_DIM)),
            pl.BlockSpec((tm, HEAD_DIM), pos_map),
            pl.BlockSpec((tm, HEAD_DIM), pos_map),
        ],
        out_specs=[
            pl.BlockSpec((tm, conv_w), lambda i: (i, 0)),
            pl.BlockSpec((tm, conv_w), lambda i: (i, 0)),
            pl.BlockSpec((n_q, HEAD_DIM, tm), lambda i: (0, 0, i)),
            pl.BlockSpec((tm, N_KV_HEADS * HEAD_DIM), lambda i: (i, 0)),
            pl.BlockSpec((N_KV_HEADS, 1, HEAD_DIM, tm), lambda i: (0, i // r, 0, i % r)),
        ],
        out_shape=[
            jax.ShapeDtypeStruct((n, conv_w), F32),
            jax.ShapeDtypeStruct((n, conv_w), F32),
            jax.ShapeDtypeStruct((n_q, HEAD_DIM, n), BF16),
            jax.ShapeDtypeStruct((n, N_KV_HEADS * HEAD_DIM), BF16),
            jax.ShapeDtypeStruct((N_KV_HEADS, n // kchunk, HEAD_DIM, kchunk), BF16),
        ],
        scratch_shapes=[pltpu.VMEM(w_in.shape, w_in.dtype), pltpu.SemaphoreType.DMA(())],
        compiler_params=_params(1),
        name="in_proj",
    )(xp, xs, g1, w_in, qg, kg, cos, sin_signed)


def _attention_kernel(qT_ref, k_ref, vT_ref, w_ref, o_ref, wb_ref, m_ref, l_ref, acc_ref, s_ref, mx_ref,
                      *, n_chunks, kchunk, ksub, group):
    wb_ref[...] = w_ref[...].astype(BF16)

    m_ref[...] = jnp.full(m_ref.shape, -jnp.inf, F32)
    l_ref[...] = jnp.zeros(l_ref.shape, F32)
    acc_ref[...] = jnp.zeros(acc_ref.shape, F32)
    sub = V7X_SUBLANES

    n_sub = kchunk // ksub

    def scores_sub(c, g, par, j, mx):
        r0 = pl.multiple_of(c * kchunk + j * ksub, ksub)
        sT = jnp.dot(k_ref[pl.ds(r0, ksub), :], qT_ref[g], preferred_element_type=F32)
        s_ref[par, j * ksub:(j + 1) * ksub, :] = sT
        cm = jnp.max(sT.reshape(ksub // sub, sub, sT.shape[1]), axis=0)
        return cm if mx is None else jnp.maximum(mx, cm)

    def consume_sub(c, par, j, m_new, pv, l):
        p = jnp.exp2(s_ref[par, j * ksub:(j + 1) * ksub, :] - m_new)
        d = jnp.dot(vT_ref[0, c, :, j * ksub:(j + 1) * ksub], p.astype(BF16), preferred_element_type=F32)
        ps = jnp.sum(p.reshape(ksub // sub, sub, p.shape[1]), axis=0)
        return (d if pv is None else pv + d), (ps if l is None else l + ps)

    def pair(c_next, g_next, par_next, c, g, par):
        m_old = m_ref[g]
        m_new = jnp.maximum(m_old, jnp.max(mx_ref[par], axis=0, keepdims=True))
        alpha = jnp.exp2(m_old - m_new)
        mx = pv = l = None
        for j in range(n_sub):
            mx = scores_sub(c_next, g_next, par_next, j, mx)
            pv, l = consume_sub(c, par, j, m_new, pv, l)
        mx_ref[par_next] = mx
        l_ref[g] = alpha * l_ref[g] + jnp.sum(l, axis=0, keepdims=True)
        acc_ref[g] = alpha * acc_ref[g] + pv
        m_ref[g] = m_new

    def parity(c, g):
        return (g % 2) if group % 2 == 0 else (c * group + g) % 2

    mx0 = None
    for j in range(n_sub):
        mx0 = scores_sub(0, 0, 0, j, mx0)
    mx_ref[0] = mx0

    def chunk(c, carry):
        for g in range(group):
            if g + 1 < group:
                pair(c, g + 1, parity(c, g + 1), c, g, parity(c, g))
            else:
                pair(jnp.minimum(c + 1, n_chunks - 1), 0, parity(c + 1, 0), c, g, parity(c, g))
        return carry

    lax.fori_loop(0, n_chunks, chunk, 0, unroll=math.gcd(n_chunks, 4))
    for g in range(group):
        o = acc_ref[g] / l_ref[g]
        o_ref[:, g * HEAD_DIM:(g + 1) * HEAD_DIM] = o.T.astype(o_ref.dtype)


def _attention(qT, k, vT, w, *, row_start, batch, seq, kchunk):
    n_q = qT.shape[0]
    group = n_q // N_KV_HEADS
    tq = _pick(256, seq)
    nqt = seq // tq
    q0 = row_start // tq
    s0 = row_start // seq
    n_chunks = seq // kchunk
    n_steps = batch * N_KV_HEADS * nqt
    w_rows, w_cols = w.shape
    wr = w_rows // n_steps
    assert wr * n_steps == w_rows and wr % (2 * V7X_SUBLANES) == 0, "weight rows must split evenly over the grid"
    step = lambda b, h, i: ((b * N_KV_HEADS + h) * nqt + i, 0)
    kernel = functools.partial(_attention_kernel, n_chunks=n_chunks, kchunk=kchunk, ksub=_pick(256, kchunk),
                               group=group)
    return pl.pallas_call(
        kernel,
        grid=(batch, N_KV_HEADS, nqt),
        in_specs=[
            pl.BlockSpec((group, HEAD_DIM, tq), lambda b, h, i: (h, 0, q0 + b * nqt + i)),
            pl.BlockSpec((seq, HEAD_DIM), lambda b, h, i: (s0 + b, h)),
            pl.BlockSpec((1, n_chunks, HEAD_DIM, kchunk), lambda b, h, i: (h, s0 + b, 0, 0)),
            pl.BlockSpec((wr, w_cols), step),
        ],
        out_specs=[
            pl.BlockSpec((tq, group * HEAD_DIM), lambda b, h, i: (b * nqt + i, h)),
            pl.BlockSpec((wr, w_cols), step),
        ],
        out_shape=[
            jax.ShapeDtypeStruct((batch * seq, n_q * HEAD_DIM), BF16),
            jax.ShapeDtypeStruct((w_rows, w_cols), BF16),
        ],
        scratch_shapes=[
            pltpu.VMEM((group, 1, tq), F32),
            pltpu.VMEM((group, 1, tq), F32),
            pltpu.VMEM((group, HEAD_DIM, tq), F32),
            pltpu.VMEM((2, kchunk, tq), F32),
            pltpu.VMEM((2, V7X_SUBLANES, tq), F32),
        ],
        compiler_params=_params(3),
        name="attention",
    )(qT, k, vT, w)


def _out_proj_kernel(xp_ref, xs_ref, gb_ref, u_ref, up_ref, un_ref, ap_ref, as_ref, cw_ref, cb_ref, wo_hbm,
                     g2_ref, rw_ref, rb_ref,
                     x1_ref, h2_ref, mi_ref, mw_ref, cnt_ref, ubuf, wo_ref, wo_sem,
                     *, n_prompt_tiles, spt, sst, conv_w, tm):
    _load_once(wo_hbm, wo_ref, wo_sem)
    i = pl.program_id(0)
    is_p = i < n_prompt_tiles
    local = jnp.where(is_p, i % spt, (i - n_prompt_tiles) % sst)
    per_seq = jnp.where(is_p, spt, sst)
    keep_prev = (local > 0).astype(F32)
    keep_next = (local < per_seq - 1).astype(F32)

    sub = V7X_SUBLANES
    ubuf[0:sub, :] = up_ref[...] * keep_prev
    ubuf[sub:sub + tm, :] = u_ref[...]
    ubuf[sub + tm:2 * sub + tm, :] = un_ref[...] * keep_next
    conv = (ubuf[sub - 1:sub - 1 + tm, :] * cw_ref[0:1, :] + ubuf[sub:sub + tm, :] * cw_ref[1:2, :]
            + ubuf[sub + 1:sub + 1 + tm, :] * cw_ref[2:3, :] + cb_ref[...])
    conv_out = (gb_ref[...] * conv).astype(BF16)
    attn = jnp.where(is_p, ap_ref[...], as_ref[...])
    x = jnp.where(is_p, xp_ref[...], xs_ref[...])
    x1 = (x + jnp.dot(conv_out, wo_ref[0:conv_w, :], preferred_element_type=F32)
          + jnp.dot(attn, wo_ref[conv_w:, :], preferred_element_type=F32))
    x1_ref[...] = x1
    h2 = _rms(x1, g2_ref[...])
    h2_ref[...] = h2

    logits = jnp.dot(h2.astype(BF16), rw_ref[...], preferred_element_type=F32) + rb_ref[...]
    lane = lax.broadcasted_iota(jnp.int32, logits.shape, 1)
    vals, hots = [], []
    work = logits
    for _ in range(TOP_K):
        m = jnp.max(work, axis=-1, keepdims=True)
        idx = jnp.min(jnp.where(work == m, lane, V7X_LANES), axis=-1, keepdims=True)
        hot = lane == idx
        vals.append(m)
        hots.append((hot, idx))
        work = jnp.where(hot, -jnp.inf, work)
    es = [jnp.exp(v - vals[0]) for v in vals]
    den = es[0] + es[1] + es[2] + es[3]

    @pl.when(i == 0)
    def _():
        cnt_ref[...] = jnp.zeros(cnt_ref.shape, F32)

    sel = jnp.zeros(logits.shape, F32)
    for hot, _ in hots:
        sel = sel + hot.astype(F32)
    row = lax.broadcasted_iota(jnp.int32, (tm, tm), 0)
    col = lax.broadcasted_iota(jnp.int32, (tm, tm), 1)
    lower = jnp.where(col < row, 1.0, 0.0).astype(BF16)
    before = jnp.dot(lower, sel.astype(BF16), preferred_element_type=F32) + cnt_ref[0:1, :]
    mi = jnp.zeros(logits.shape, jnp.int32)
    mw = jnp.zeros(logits.shape, F32)
    for kk, (hot, idx) in enumerate(hots):
        rank = jnp.sum(jnp.where(hot, before, 0.0), axis=-1, keepdims=True).astype(jnp.int32)
        mi = jnp.where(lane == kk, idx, mi)
        mi = jnp.where(lane == TOP_K + kk, rank, mi)
        mw = jnp.where(lane == kk, es[kk] / den, mw)
    mi_ref[...] = mi
    mw_ref[...] = mw
    cnt_ref[...] = cnt_ref[...] + jnp.sum(sel, axis=0, keepdims=True)


def _out_proj(xp, xs, gb, u, attn_p, attn_s, conv_w_, conv_b, w_out, g2, rw, rb, *, s_p, s_s):
    n_p, d = xp.shape
    n_s = xs.shape[0]
    n = n_p + n_s
    conv_w = gb.shape[1]
    a_w = attn_p.shape[1]
    tm = _pick(256, math.gcd(s_p, s_s))
    npt, spt, sst = n_p // tm, s_p // tm, s_s // tm
    sub = V7X_SUBLANES
    hb = tm // sub
    kernel = functools.partial(_out_proj_kernel, n_prompt_tiles=npt, spt=spt, sst=sst, conv_w=conv_w, tm=tm)
    row = lambda i: (i, 0)
    return pl.pallas_call(
        kernel,
        grid=(n // tm,),
        in_specs=[
            pl.BlockSpec((tm, d), lambda i: (jnp.minimum(i, npt - 1), 0)),
            pl.BlockSpec((tm, d), lambda i: (jnp.maximum(i - npt, 0), 0)),
            pl.BlockSpec((tm, conv_w), row),
            pl.BlockSpec((tm, conv_w), row),
            pl.BlockSpec((sub, conv_w), lambda i: (jnp.maximum(i * hb - 1, 0), 0)),
            pl.BlockSpec((sub, conv_w), lambda i: (jnp.minimum((i + 1) * hb, n // sub - 1), 0)),
            pl.BlockSpec((tm, a_w), lambda i: (jnp.minimum(i, npt - 1), 0)),
            pl.BlockSpec((tm, a_w), lambda i: (jnp.maximum(i - npt, 0), 0)),
            _resident(conv_w_.shape),
            _resident((1, conv_w)),
            pl.BlockSpec(memory_space=pl.ANY),
            _resident((1, d)),
            _resident(rw.shape),
            _resident(rb.shape),
        ],
        out_specs=[
            pl.BlockSpec((tm, d), row),
            pl.BlockSpec((tm, d), row),
            pl.BlockSpec((tm, V7X_LANES), row),
            pl.BlockSpec((tm, V7X_LANES), row),
            pl.BlockSpec((V7X_SUBLANES, V7X_LANES), lambda i: (0, 0)),
        ],
        out_shape=[
            jax.ShapeDtypeStruct((n, d), F32),
            jax.ShapeDtypeStruct((n, d), F32),
            jax.ShapeDtypeStruct((n, V7X_LANES), jnp.int32),
            jax.ShapeDtypeStruct((n, V7X_LANES), F32),
            jax.ShapeDtypeStruct((V7X_SUBLANES, V7X_LANES), F32),
        ],
        scratch_shapes=[pltpu.VMEM((tm + 2 * sub, conv_w), F32), pltpu.VMEM(w_out.shape, w_out.dtype),
                        pltpu.SemaphoreType.DMA(())],
        compiler_params=_params(1),
        name="out_proj",
    )(xp, xs, gb, u, u, u, attn_p, attn_s, conv_w_, conv_b, w_out, g2, rw, rb)


def _dispatch_kernel(pos_ref, h_hbm, xs_hbm, buf, sem_in, sem_out, *, tm, n_tiles):
    i = pl.program_id(0)
    slot = i % 2

    def load(tile, s):
        return pltpu.make_async_copy(h_hbm.at[pl.ds(tile * tm, tm)], buf.at[s], sem_in.at[s])

    def row_out(s, t, dst_row):
        return pltpu.make_async_copy(buf.at[s, pl.ds(t, 1)], xs_hbm.at[pl.ds(dst_row, 1)], sem_out.at[s])

    def drain(s):
        for _ in range(TOP_K):
            pltpu.make_async_copy(buf.at[s], xs_hbm.at[pl.ds(0, tm)], sem_out.at[s]).wait()

    @pl.when(i == 0)
    def _():
        load(0, 0).start()

    @pl.when(i > 0)
    def _():
        drain(1 - slot)

    @pl.when(i + 1 < n_tiles)
    def _():
        load(i + 1, 1 - slot).start()

    load(i, slot).wait()

    def issue(t, carry):
        for kk in range(TOP_K):
            row_out(slot, t, pos_ref[0, 0, t * TOP_K + kk]).start(priority=kk % 2)
        return carry

    lax.fori_loop(0, tm, issue, 0, unroll=8)

    @pl.when(i == n_tiles - 1)
    def _():
        drain(slot)


def _dispatch(pos, h2):
    n, d = h2.shape
    tm = _pick(256, n)
    n_tiles = n // tm
    pos3 = pos.reshape(n_tiles, 1, tm * TOP_K)
    kernel = functools.partial(_dispatch_kernel, tm=tm, n_tiles=n_tiles)
    return pl.pallas_call(
        kernel,
        grid=(n_tiles,),
        in_specs=[
            pl.BlockSpec((1, 1, tm * TOP_K), lambda i: (i, 0, 0), memory_space=pltpu.SMEM),
            pl.BlockSpec(memory_space=pl.ANY),
        ],
        out_specs=pl.BlockSpec(memory_space=pl.ANY),
        out_shape=jax.ShapeDtypeStruct((n * TOP_K, d), h2.dtype),
        scratch_shapes=[pltpu.VMEM((2, tm, d), h2.dtype), pltpu.SemaphoreType.DMA((2,)),
                        pltpu.SemaphoreType.DMA((2,))],
        compiler_params=_params(1),
        name="dispatch",
    )(pos3, h2)


def _experts_kernel(vt_ref, ve_ref, vfirst_ref, vlo_ref, vhi_ref,
                    x_ref, wg_ref, wu_ref, bg_ref, bu_ref, wd_ref, bd_ref, o_ref, act_ref, *, nf, tf):
    v = pl.program_id(0)
    f = pl.program_id(1)
    lo = vlo_ref[v]
    hi = vhi_ref[v]

    @pl.when(hi > lo)
    def _():
        xb = x_ref[...].astype(BF16)
        gate = jnp.dot(xb, wg_ref[0], preferred_element_type=F32) + bg_ref[0]
        up = jnp.dot(xb, wu_ref[0], preferred_element_type=F32) + bu_ref[0]
        gate = jnp.minimum(gate, SWIGLU_LIMIT)
        up = jnp.clip(up, -SWIGLU_LIMIT, SWIGLU_LIMIT)
        glu = gate * jax.nn.sigmoid(gate * SWIGLU_ALPHA)
        rows = lax.broadcasted_iota(jnp.int32, (xb.shape[0], 1), 0)
        mine = (rows >= lo) & (rows < hi)
        act_ref[f] = jnp.where(mine, (up + 1.0) * glu, 0.0).astype(BF16)

        def down():
            y = jnp.dot(act_ref[0], wd_ref[0, 0:tf, :], preferred_element_type=F32)
            for j in range(1, nf):
                y = y + jnp.dot(act_ref[j], wd_ref[0, j * tf:(j + 1) * tf, :], preferred_element_type=F32)
            return y + jnp.where(mine, bd_ref[0], 0.0)

        last = f == nf - 1
        fresh = vfirst_ref[v] == 1

        @pl.when(jnp.logical_and(last, fresh))
        def _():
            o_ref[...] = down()

        @pl.when(jnp.logical_and(last, jnp.logical_not(fresh)))
        def _():
            o_ref[...] = o_ref[...] + down()


def _visit_plan(counts, n_rows, tm, n_visits):
    n_exp = counts.shape[0]
    ends = jnp.cumsum(counts)
    starts = ends - counts
    first_tile = starts // tm
    last_tile = jnp.maximum(ends - 1, 0) // tm
    nvis = jnp.where(counts > 0, last_tile - first_tile + 1, 0)
    vis_end = jnp.cumsum(nvis)
    vis_start = vis_end - nvis
    total = vis_end[-1]
    v = jnp.arange(n_visits, dtype=jnp.int32)
    e = jnp.sum((v[:, None] >= vis_end[None, :]).astype(jnp.int32), axis=1)
    e = jnp.minimum(e, n_exp - 1)
    valid = v < total
    onehot = (e[:, None] == jnp.arange(n_exp, dtype=jnp.int32)[None, :]).astype(jnp.int32)
    pickv = lambda a: jnp.sum(onehot * a[None, :], axis=1)
    tile = pickv(first_tile) + (v - pickv(vis_start))
    lo = jnp.maximum(pickv(starts), tile * tm) - tile * tm
    hi = jnp.minimum(pickv(ends), (tile + 1) * tm) - tile * tm
    last_e = jnp.sum((total - 1 >= vis_end).astype(jnp.int32))
    tile = jnp.where(valid, tile, n_rows // tm - 1)
    e = jnp.where(valid, e, jnp.minimum(last_e, n_exp - 1))
    lo = jnp.where(valid, lo, 0)
    hi = jnp.where(valid, hi, 0)
    first = jnp.where(valid & (lo == 0), 1, 0)
    i32 = lambda a: a.astype(jnp.int32)
    return i32(tile), i32(e), i32(first), i32(lo), i32(hi)


def _experts(plan, xs, w_gu, b_gu, w_d, b_d):
    n_rows, d = xs.shape
    n_exp, _, two_f = w_gu.shape
    d_ff = two_f // 2
    tm = _pick(512, n_rows)
    tf = _pick(1024, d_ff)
    nf = d_ff // tf
    n_visits = plan[0].shape[0]

    def fx(f, hi, lo, v):
        return jnp.where(hi[v] > lo[v], f, 0)

    grid_spec = pltpu.PrefetchScalarGridSpec(
        num_scalar_prefetch=5,
        grid=(n_visits, nf),
        in_specs=[
            pl.BlockSpec((tm, d), lambda v, f, vt, ve, vf, lo, hi: (vt[v], 0)),
            pl.BlockSpec((1, d, tf), lambda v, f, vt, ve, vf, lo, hi: (ve[v], 0, fx(f, hi, lo, v))),
            pl.BlockSpec((1, d, tf), lambda v, f, vt, ve, vf, lo, hi: (ve[v], 0, nf + fx(f, hi, lo, v))),
            pl.BlockSpec((1, 1, tf), lambda v, f, vt, ve, vf, lo, hi: (ve[v], 0, fx(f, hi, lo, v))),
            pl.BlockSpec((1, 1, tf), lambda v, f, vt, ve, vf, lo, hi: (ve[v], 0, nf + fx(f, hi, lo, v))),
            pl.BlockSpec((1, d_ff, d), lambda v, f, vt, ve, vf, lo, hi: (ve[v], 0, 0)),
            pl.BlockSpec((1, 1, d), lambda v, f, vt, ve, vf, lo, hi: (ve[v], 0, 0)),
        ],
        out_specs=pl.BlockSpec((tm, d), lambda v, f, vt, ve, vf, lo, hi: (vt[v], 0)),
        scratch_shapes=[pltpu.VMEM((nf, tm, tf), BF16)],
    )
    return pl.pallas_call(
        functools.partial(_experts_kernel, nf=nf, tf=tf),
        grid_spec=grid_spec,
        out_shape=jax.ShapeDtypeStruct((n_rows, d), F32),
        compiler_params=_params(2, vmem=V7X_VMEM_BYTES - 4 * 1024 * 1024),
        name="experts",
    )(*plan, xs, w_gu, w_gu, b_gu, b_gu, w_d, b_d)


def _combine_kernel(pos_ref, nxt_ref, y_hbm, x1_ref, w_ref, fg_ref, o_ref, buf, sem, *, tm, n_steps):
    i = pl.program_id(0)
    slot = i % 2

    def fetch(p_ref, s):
        def body(t, carry):
            for kk in range(TOP_K):
                pltpu.make_async_copy(y_hbm.at[pl.ds(p_ref[0, 0, t * TOP_K + kk], 1)],
                                      buf.at[s, kk, pl.ds(t, 1)], sem.at[s]).start(priority=kk % 2)
            return carry
        lax.fori_loop(0, tm, body, 0, unroll=8)

    @pl.when(i == 0)
    def _():
        fetch(pos_ref, 0)

    @pl.when(i + 1 < n_steps)
    def _():
        fetch(nxt_ref, 1 - slot)

    for kk in range(TOP_K):
        pltpu.make_async_copy(y_hbm.at[pl.ds(0, tm)], buf.at[slot, kk], sem.at[slot]).wait()
    acc = x1_ref[...]
    for kk in range(TOP_K):
        acc = acc + w_ref[:, kk:kk + 1] * buf[slot, kk]
    o_ref[...] = _rms(acc, fg_ref[...])


def _combine(pos, y_sorted, x1, wts, fg, *, row_start, rows):
    n, d = x1.shape
    tm = _pick(128, rows)
    n_steps = rows // tm
    t0 = row_start // tm
    pos3 = pos.reshape(n // tm, 1, tm * TOP_K)
    kernel = functools.partial(_combine_kernel, tm=tm, n_steps=n_steps)
    smem = lambda im: pl.BlockSpec((1, 1, tm * TOP_K), im, memory_space=pltpu.SMEM)
    return pl.pallas_call(
        kernel,
        grid=(n_steps,),
        in_specs=[
            smem(lambda i: (t0 + i, 0, 0)),
            smem(lambda i: (t0 + jnp.minimum(i + 1, n_steps - 1), 0, 0)),
            pl.BlockSpec(memory_space=pl.ANY),
            pl.BlockSpec((tm, d), lambda i: (t0 + i, 0)),
            pl.BlockSpec((tm, V7X_LANES), lambda i: (t0 + i, 0)),
            _resident((1, d)),
        ],
        out_specs=pl.BlockSpec((tm, d), lambda i: (i, 0)),
        out_shape=jax.ShapeDtypeStruct((rows, d), F32),
        scratch_shapes=[pltpu.VMEM((2, TOP_K, tm, d), F32), pltpu.SemaphoreType.DMA((2,))],
        compiler_params=_params(1),
        name="combine",
    )(pos3, pos3, y_sorted, x1, wts, fg)


def _rope_tables(n_tokens):
    rows = n_tokens // GRID_W
    axis_dim = HEAD_DIM // 2
    row = jnp.repeat(jnp.arange(rows, dtype=F32), GRID_W)
    col = jnp.tile(jnp.arange(GRID_W, dtype=F32), rows)
    inv_freq = ROPE_THETA ** (-jnp.arange(0, axis_dim, 2, dtype=F32) / axis_dim)
    ang_r = row[:, None] * inv_freq[None, :]
    ang_c = col[:, None] * inv_freq[None, :]
    ang = jnp.concatenate([ang_r, ang_r, ang_c, ang_c], axis=-1)
    quarter = jnp.arange(HEAD_DIM) % (HEAD_DIM // 2) < HEAD_DIM // 4
    return jnp.cos(ang), jnp.where(quarter[None, :], -1.0, 1.0).astype(F32) * jnp.sin(ang)


def kernel(x_prompt, x_sample, norm1_g, w_in, conv_w, conv_b, q_norm_g, k_norm_g, w_out, norm2_g,
           router_w, router_b, w_gate_up, b_gate_up, w_down, b_down, final_g):
    assert norm1_g.shape[0] == 1, "one layer"
    b_p, s_p, d = x_prompt.shape
    b_s, s_s, _ = x_sample.shape
    n_p, n_s = b_p * s_p, b_s * s_s
    n = n_p + n_s
    n_exp = router_w.shape[-1]
    assert n_exp <= V7X_LANES and n_p % s_s == 0
    kchunk = _pick(1024, math.gcd(s_p, s_s))
    xp = x_prompt.reshape(n_p, d)
    xs = x_sample.reshape(n_s, d)
    cos, sin_signed = _rope_tables(max(s_p, s_s))

    gb, u, qT, k, vT = _in_proj(
        xp, xs, norm1_g[0][None, :], w_in[0].astype(BF16), q_norm_g[0][None, :], k_norm_g[0][None, :],
        cos, sin_signed, s_p=s_p, s_s=s_s, kchunk=kchunk)
    wgu, wd = w_gate_up[0], w_down[0]
    attn_p, wgu_b = _attention(qT, k, vT, wgu.reshape(-1, wgu.shape[-1]), row_start=0, batch=b_p, seq=s_p,
                               kchunk=kchunk)
    attn_s, wd_b = _attention(qT, k, vT, wd.reshape(-1, wd.shape[-1]), row_start=n_p, batch=b_s, seq=s_s,
                              kchunk=kchunk)

    rw = jnp.zeros((d, V7X_LANES), BF16).at[:, :n_exp].set(router_w[0].astype(BF16))
    rb = jnp.full((1, V7X_LANES), NEG_PAD, F32).at[0, :n_exp].set(router_b[0])
    x1, h2, meta_i, meta_w, cnt = _out_proj(
        xp, xs, gb, u, attn_p, attn_s, conv_w[0], conv_b[0][None, :], w_out[0].astype(BF16),
        norm2_g[0][None, :], rw, rb, s_p=s_p, s_s=s_s)

    counts = cnt[0, :n_exp].astype(jnp.int32)
    starts = jnp.cumsum(counts) - counts
    ids = meta_i[:, :TOP_K]
    ranks = meta_i[:, TOP_K:2 * TOP_K]
    onehot = ids[:, :, None] == jnp.arange(n_exp, dtype=jnp.int32)[None, None, :]
    pos = ranks + jnp.sum(jnp.where(onehot, starts[None, None, :], 0), axis=-1)

    x_sorted = _dispatch(pos, h2)
    n_rows = n * TOP_K
    tm_e = _pick(512, n_rows)
    plan = _visit_plan(counts, n_rows, tm_e, n_rows // tm_e + n_exp - 1)
    y_sorted = _experts(plan, x_sorted, wgu_b.reshape(wgu.shape), b_gate_up[0][:, None, :],
                        wd_b.reshape(wd.shape), b_down[0][:, None, :])

    fg = final_g[None, :]
    y_p = _combine(pos, y_sorted, x1, meta_w, fg, row_start=0, rows=n_p)
    y_s = _combine(pos, y_sorted, x1, meta_w, fg, row_start=n_p, rows=n_s)
    return (y_p.reshape(b_p, s_p, d), y_s.reshape(b_s, s_s, d))
```

```python
import functools
import math

import jax
import jax.numpy as jnp
from jax import lax
from jax.experimental import pallas as pl
from jax.experimental.pallas import tpu as pltpu

HEAD_DIM = 128
N_KV_HEADS = 2
GRID_W = 64
ROPE_THETA = 10000.0
TOP_K = 4
SWIGLU_LIMIT = 7.0
SWIGLU_ALPHA = 1.702
NORM_EPS = 1e-6

V7X_LANES = 128
V7X_SUBLANES = 8
V7X_VMEM_BYTES = 64 * 1024 * 1024
VMEM_LIMIT = V7X_VMEM_BYTES - 8 * 1024 * 1024
NEG_PAD = -1e30
OUT_COL_CHUNKS = 8

F32 = jnp.float32
BF16 = jnp.bfloat16
LOG2E = math.log2(math.e)


def _pick(target, n):
    t = min(target, n)
    while n % t:
        t //= 2
    return t


def _params(n_axes, vmem=VMEM_LIMIT):
    return pltpu.CompilerParams(dimension_semantics=("arbitrary",) * n_axes, vmem_limit_bytes=vmem)


def _resident(shape):
    nd = len(shape)
    return pl.BlockSpec(shape, lambda *_: (0,) * nd)


def _load_once(src_hbm, dst_vmem, sem):
    @pl.when(pl.program_id(0) == 0)
    def _():
        copy = pltpu.make_async_copy(src_hbm, dst_vmem, sem)
        copy.start()
        copy.wait()


def _rms(x, g):
    ms = jnp.mean(x * x, axis=-1, keepdims=True)
    return x * lax.rsqrt(ms + NORM_EPS) * g


def _in_proj_kernel(xp_ref, xs_ref, g1_ref, w_hbm, qg_ref, kg_ref, cos_ref, sin_ref,
                    gb_ref, u_ref, qT_ref, k_ref, vT_ref, w_ref, w_sem, *, n_prompt_tiles, conv_w, n_q, q_scale):
    _load_once(w_hbm, w_ref, w_sem)
    i = pl.program_id(0)
    x = jnp.where(i < n_prompt_tiles, xp_ref[...], xs_ref[...])
    h = _rms(x, g1_ref[...]).astype(BF16)

    def proj(c0, c1):
        return jnp.dot(h, w_ref[:, c0:c1], preferred_element_type=F32)

    cos = cos_ref[...]
    sin = sin_ref[...]
    lane = lax.broadcasted_iota(jnp.int32, (1, HEAD_DIM), 1)
    first_half = (lane % (HEAD_DIM // 2)) < (HEAD_DIM // 4)

    def norm_rope(xh, g):
        y = _rms(xh, g)
        up = pltpu.roll(y, HEAD_DIM - HEAD_DIM // 4, 1)
        dn = pltpu.roll(y, HEAD_DIM // 4, 1)
        return y * cos + jnp.where(first_half, up, dn) * sin

    q0 = 3 * conv_w
    k0 = q0 + n_q * HEAD_DIM
    q = proj(q0, k0)
    kv = proj(k0, k0 + 2 * N_KV_HEADS * HEAD_DIM)

    def q_head(hd):
        qh = norm_rope(q[:, hd * HEAD_DIM:(hd + 1) * HEAD_DIM], qg_ref[...]) * q_scale
        qT_ref[hd] = qh.T.astype(BF16)

    def kv_head(hd):
        kh = norm_rope(kv[:, hd * HEAD_DIM:(hd + 1) * HEAD_DIM], kg_ref[...])
        k_ref[:, hd * HEAD_DIM:(hd + 1) * HEAD_DIM] = kh.astype(BF16)
        vh = kv[:, (N_KV_HEADS + hd) * HEAD_DIM:(N_KV_HEADS + hd + 1) * HEAD_DIM]
        vT_ref[hd, 0] = vh.T.astype(BF16)

    head_work = ([functools.partial(q_head, hd) for hd in range(n_q)]
                 + [functools.partial(kv_head, hd) for hd in range(N_KV_HEADS)])
    cw = HEAD_DIM * 2
    n_cc = conv_w // cw
    for c in range(n_cc):
        cols = slice(c * cw, (c + 1) * cw)
        gb_ref[:, cols] = proj(c * cw, (c + 1) * cw)
        if 2 * c < len(head_work):
            head_work[2 * c]()
        u_ref[:, cols] = proj(conv_w + c * cw, conv_w + (c + 1) * cw) * proj(2 * conv_w + c * cw,
                                                                            2 * conv_w + (c + 1) * cw)
        if 2 * c + 1 < len(head_work):
            head_work[2 * c + 1]()
    for work in head_work[2 * n_cc:]:
        work()


def _in_proj(xp, xs, g1, w_in, qg, kg, cos, sin_signed, *, s_p, s_s, kchunk):
    n_p, d = xp.shape
    n_s = xs.shape[0]
    n = n_p + n_s
    conv_w = d // 2
    n_q = (d - conv_w) // HEAD_DIM
    tm = _pick(256, math.gcd(s_p, s_s))
    npt, spt, sst = n_p // tm, s_p // tm, s_s // tm
    r = kchunk // tm
    q_scale = LOG2E / math.sqrt(HEAD_DIM)

    def pos_map(i):
        return (jnp.where(i < npt, i % spt, (i - npt) % sst), 0)

    kernel = functools.partial(_in_proj_kernel, n_prompt_tiles=npt, conv_w=conv_w, n_q=n_q, q_scale=q_scale)
    return pl.pallas_call(
        kernel,
        grid=(n // tm,),
        in_specs=[
            pl.BlockSpec((tm, d), lambda i: (jnp.minimum(i, npt - 1), 0)),
            pl.BlockSpec((tm, d), lambda i: (jnp.maximum(i - npt, 0), 0)),
            _resident((1, d)),
            pl.BlockSpec(memory_space=pl.ANY),
            _resident((1, HEAD_DIM)),
            _resident((1, HEAD_DIM)),
            pl.BlockSpec((tm, HEAD_DIM), pos_map),
            pl.BlockSpec((tm, HEAD_DIM), pos_map),
        ],
        out_specs=[
            pl.BlockSpec((tm, conv_w), lambda i: (i, 0)),
            pl.BlockSpec((tm, conv_w), lambda i: (i, 0)),
            pl.BlockSpec((n_q, HEAD_DIM, tm), lambda i: (0, 0, i)),
            pl.BlockSpec((tm, N_KV_HEADS * HEAD_DIM), lambda i: (i, 0)),
            pl.BlockSpec((N_KV_HEADS, 1, HEAD_DIM, tm), lambda i: (0, i // r, 0, i % r)),
        ],
        out_shape=[
            jax.ShapeDtypeStruct((n, conv_w), F32),
            jax.ShapeDtypeStruct((n, conv_w), F32),
            jax.ShapeDtypeStruct((n_q, HEAD_DIM, n), BF16),
            jax.ShapeDtypeStruct((n, N_KV_HEADS * HEAD_DIM), BF16),
            jax.ShapeDtypeStruct((N_KV_HEADS, n // kchunk, HEAD_DIM, kchunk), BF16),
        ],
        scratch_shapes=[pltpu.VMEM(w_in.shape, w_in.dtype), pltpu.SemaphoreType.DMA(())],
        compiler_params=_params(1),
        name="in_proj",
    )(xp, xs, g1, w_in, qg, kg, cos, sin_signed)


def _attention_kernel(qT_ref, k_ref, vT_ref, w_ref, o_ref, wb_ref, m_ref, l_ref, acc_ref, s_ref, mx_ref,
                      *, n_chunks, kchunk, ksub, group):
    wb_ref[...] = w_ref[...].astype(BF16)

    m_ref[...] = jnp.full(m_ref.shape, -jnp.inf, F32)
    l_ref[...] = jnp.zeros(l_ref.shape, F32)
    acc_ref[...] = jnp.zeros(acc_ref.shape, F32)
    sub = V7X_SUBLANES

    n_sub = kchunk // ksub

    def scores_sub(c, g, par, j, mx):
        r0 = pl.multiple_of(c * kchunk + j * ksub, ksub)
        sT = jnp.dot(k_ref[pl.ds(r0, ksub), :], qT_ref[g], preferred_element_type=F32)
        s_ref[par, j * ksub:(j + 1) * ksub, :] = sT
        cm = jnp.max(sT.reshape(ksub // sub, sub, sT.shape[1]), axis=0)
        return cm if mx is None else jnp.maximum(mx, cm)

    def consume_sub(c, par, j, m_new, pv, l):
        p = jnp.exp2(s_ref[par, j * ksub:(j + 1) * ksub, :] - m_new)
        d = jnp.dot(vT_ref[0, c, :, j * ksub:(j + 1) * ksub], p.astype(BF16), preferred_element_type=F32)
        ps = jnp.sum(p.reshape(ksub // sub, sub, p.shape[1]), axis=0)
        return (d if pv is None else pv + d), (ps if l is None else l + ps)

    def pair(c_next, g_next, par_next, c, g, par):
        m_old = m_ref[g]
        m_new = jnp.maximum(m_old, jnp.max(mx_ref[par], axis=0, keepdims=True))
        alpha = jnp.exp2(m_old - m_new)
        mx = pv = l = None
        for j in range(n_sub):
            mx = scores_sub(c_next, g_next, par_next, j, mx)
            pv, l = consume_sub(c, par, j, m_new, pv, l)
        mx_ref[par_next] = mx
        l_ref[g] = alpha * l_ref[g] + jnp.sum(l, axis=0, keepdims=True)
        acc_ref[g] = alpha * acc_ref[g] + pv
        m_ref[g] = m_new

    def parity(c, g):
        return (g % 2) if group % 2 == 0 else (c * group + g) % 2

    mx0 = None
    for j in range(n_sub):
        mx0 = scores_sub(0, 0, 0, j, mx0)
    mx_ref[0] = mx0

    def chunk(c, carry):
        for g in range(group):
            if g + 1 < group:
                pair(c, g + 1, parity(c, g + 1), c, g, parity(c, g))
            else:
                pair(jnp.minimum(c + 1, n_chunks - 1), 0, parity(c + 1, 0), c, g, parity(c, g))
        return carry

    lax.fori_loop(0, n_chunks, chunk, 0, unroll=math.gcd(n_chunks, 4))
    for g in range(group):
        o = acc_ref[g] / l_ref[g]
        o_ref[:, g * HEAD_DIM:(g + 1) * HEAD_DIM] = o.T.astype(o_ref.dtype)


def _attention(qT, k, vT, w, *, row_start, batch, seq, kchunk):
    n_q = qT.shape[0]
    group = n_q // N_KV_HEADS
    tq = _pick(256, seq)
    nqt = seq // tq
    q0 = row_start // tq
    s0 = row_start // seq
    n_chunks = seq // kchunk
    n_steps = batch * N_KV_HEADS * nqt
    w_rows, w_cols = w.shape
    wr = w_rows // n_steps
    assert wr * n_steps == w_rows and wr % (2 * V7X_SUBLANES) == 0, "weight rows must split evenly over the grid"
    step = lambda b, h, i: ((b * N_KV_HEADS + h) * nqt + i, 0)
    kernel = functools.partial(_attention_kernel, n_chunks=n_chunks, kchunk=kchunk, ksub=_pick(256, kchunk),
                               group=group)
    return pl.pallas_call(
        kernel,
        grid=(batch, N_KV_HEADS, nqt),
        in_specs=[
            pl.BlockSpec((group, HEAD_DIM, tq), lambda b, h, i: (h, 0, q0 + b * nqt + i)),
            pl.BlockSpec((seq, HEAD_DIM), lambda b, h, i: (s0 + b, h)),
            pl.BlockSpec((1, n_chunks, HEAD_DIM, kchunk), lambda b, h, i: (h, s0 + b, 0, 0)),
            pl.BlockSpec((wr, w_cols), step),
        ],
        out_specs=[
            pl.BlockSpec((tq, group * HEAD_DIM), lambda b, h, i: (b * nqt + i, h)),
            pl.BlockSpec((wr, w_cols), step),
        ],
        out_shape=[
            jax.ShapeDtypeStruct((batch * seq, n_q * HEAD_DIM), BF16),
            jax.ShapeDtypeStruct((w_rows, w_cols), BF16),
        ],
        scratch_shapes=[
            pltpu.VMEM((group, 1, tq), F32),
            pltpu.VMEM((group, 1, tq), F32),
            pltpu.VMEM((group, HEAD_DIM, tq), F32),
            pltpu.VMEM((2, kchunk, tq), F32),
            pltpu.VMEM((2, V7X_SUBLANES, tq), F32),
        ],
        compiler_params=_params(3),
        name="attention",
    )(qT, k, vT, w)


def _out_proj_kernel(xp_ref, xs_ref, gb_ref, u_ref, up_ref, un_ref, ap_ref, as_ref, cw_ref, cb_ref, wo_hbm,
                     g2_ref, rw_ref, rb_ref,
                     x1_ref, h2_ref, mi_ref, mw_ref, cnt_ref, ubuf, wo_ref, wo_sem, hprev,
                     *, n_prompt_tiles, n_tiles, spt, sst, conv_w, tm):
    _load_once(wo_hbm, wo_ref, wo_sem)
    step = pl.program_id(0)

    @pl.when(step == 0)
    def _():
        cnt_ref[...] = jnp.zeros(cnt_ref.shape, F32)
        hprev[...] = jnp.zeros(hprev.shape, hprev.dtype)

    routed = hprev[...]
    i = jnp.minimum(step, n_tiles - 1)
    is_p = i < n_prompt_tiles
    local = jnp.where(is_p, i % spt, (i - n_prompt_tiles) % sst)
    per_seq = jnp.where(is_p, spt, sst)
    keep_prev = (local > 0).astype(F32)
    keep_next = (local < per_seq - 1).astype(F32)

    sub = V7X_SUBLANES
    ubuf[0:sub, :] = up_ref[...] * keep_prev
    ubuf[sub:sub + tm, :] = u_ref[...]
    ubuf[sub + tm:2 * sub + tm, :] = un_ref[...] * keep_next
    conv = (ubuf[sub - 1:sub - 1 + tm, :] * cw_ref[0:1, :] + ubuf[sub:sub + tm, :] * cw_ref[1:2, :]
            + ubuf[sub + 1:sub + 1 + tm, :] * cw_ref[2:3, :] + cb_ref[...])
    conv_out = (gb_ref[...] * conv).astype(BF16)
    attn = jnp.where(is_p, ap_ref[...], as_ref[...])
    logits = jnp.dot(routed, rw_ref[...], preferred_element_type=F32) + rb_ref[...]
    lane = lax.broadcasted_iota(jnp.int32, logits.shape, 1)
    st = dict(work=logits, vals=[], hots=[])

    def select_round():
        work = st["work"]
        m = jnp.max(work, axis=-1, keepdims=True)
        idx = jnp.min(jnp.where(work == m, lane, V7X_LANES), axis=-1, keepdims=True)
        hot = lane == idx
        st["vals"].append(m)
        st["hots"].append((hot, idx))
        st["work"] = jnp.where(hot, -jnp.inf, work)

    def count_before():
        sel = jnp.zeros(logits.shape, F32)
        for hot, _ in st["hots"]:
            sel = sel + hot.astype(F32)
        sel = sel * (step > 0).astype(F32)
        row = lax.broadcasted_iota(jnp.int32, (tm, tm), 0)
        col = lax.broadcasted_iota(jnp.int32, (tm, tm), 1)
        lower = jnp.where(col < row, 1.0, 0.0).astype(BF16)
        st["before"] = jnp.dot(lower, sel.astype(BF16), preferred_element_type=F32) + cnt_ref[0:1, :]
        cnt_ref[...] = cnt_ref[...] + jnp.sum(sel, axis=0, keepdims=True)

    def write_meta():
        vals = st["vals"]
        es = [jnp.exp(v - vals[0]) for v in vals]
        den = es[0] + es[1] + es[2] + es[3]
        mi = jnp.zeros(logits.shape, jnp.int32)
        mw = jnp.zeros(logits.shape, F32)
        for kk, (hot, idx) in enumerate(st["hots"]):
            rank = jnp.sum(jnp.where(hot, st["before"], 0.0), axis=-1, keepdims=True).astype(jnp.int32)
            mi = jnp.where(lane == kk, idx, mi)
            mi = jnp.where(lane == TOP_K + kk, rank, mi)
            mw = jnp.where(lane == kk, es[kk] / den, mw)
        mi_ref[...] = mi
        mw_ref[...] = mw

    router_stages = [select_round] * TOP_K + [count_before, write_meta]
    d_out = wo_ref.shape[1]
    cw = d_out // OUT_COL_CHUNKS
    ssq = None
    for c in range(OUT_COL_CHUNKS):
        cols = slice(c * cw, (c + 1) * cw)
        x_c = jnp.where(is_p, xp_ref[:, cols], xs_ref[:, cols])
        x1_c = (x_c + jnp.dot(conv_out, wo_ref[0:conv_w, cols], preferred_element_type=F32)
                + jnp.dot(attn, wo_ref[conv_w:, cols], preferred_element_type=F32))
        x1_ref[:, cols] = x1_c
        part = jnp.sum(x1_c * x1_c, axis=-1, keepdims=True)
        ssq = part if ssq is None else ssq + part
        if c < len(router_stages):
            router_stages[c]()
    for stage in router_stages[OUT_COL_CHUNKS:]:
        stage()
    h2 = x1_ref[...] * lax.rsqrt(ssq / d_out + NORM_EPS) * g2_ref[...]
    h2_ref[...] = h2
    hprev[...] = h2.astype(BF16)


def _out_proj(xp, xs, gb, u, attn_p, attn_s, conv_w_, conv_b, w_out, g2, rw, rb, *, s_p, s_s):
    n_p, d = xp.shape
    n_s = xs.shape[0]
    n = n_p + n_s
    conv_w = gb.shape[1]
    a_w = attn_p.shape[1]
    tm = _pick(256, math.gcd(s_p, s_s))
    npt, spt, sst = n_p // tm, s_p // tm, s_s // tm
    sub = V7X_SUBLANES
    hb = tm // sub
    nt = n // tm
    kernel = functools.partial(_out_proj_kernel, n_prompt_tiles=npt, n_tiles=nt, spt=spt, sst=sst, conv_w=conv_w,
                               tm=tm)
    tile = lambda s: jnp.minimum(s, nt - 1)
    row = lambda s: (tile(s), 0)
    routed_row = lambda s: (jnp.maximum(s - 1, 0), 0)
    return pl.pallas_call(
        kernel,
        grid=(nt + 1,),
        in_specs=[
            pl.BlockSpec((tm, d), lambda s: (jnp.minimum(tile(s), npt - 1), 0)),
            pl.BlockSpec((tm, d), lambda s: (jnp.maximum(tile(s) - npt, 0), 0)),
            pl.BlockSpec((tm, conv_w), row),
            pl.BlockSpec((tm, conv_w), row),
            pl.BlockSpec((sub, conv_w), lambda s: (jnp.maximum(tile(s) * hb - 1, 0), 0)),
            pl.BlockSpec((sub, conv_w), lambda s: (jnp.minimum((tile(s) + 1) * hb, n // sub - 1), 0)),
            pl.BlockSpec((tm, a_w), lambda s: (jnp.minimum(tile(s), npt - 1), 0)),
            pl.BlockSpec((tm, a_w), lambda s: (jnp.maximum(tile(s) - npt, 0), 0)),
            _resident(conv_w_.shape),
            _resident((1, conv_w)),
            pl.BlockSpec(memory_space=pl.ANY),
            _resident((1, d)),
            _resident(rw.shape),
            _resident(rb.shape),
        ],
        out_specs=[
            pl.BlockSpec((tm, d), row),
            pl.BlockSpec((tm, d), row),
            pl.BlockSpec((tm, V7X_LANES), routed_row),
            pl.BlockSpec((tm, V7X_LANES), routed_row),
            pl.BlockSpec((V7X_SUBLANES, V7X_LANES), lambda s: (0, 0)),
        ],
        out_shape=[
            jax.ShapeDtypeStruct((n, d), F32),
            jax.ShapeDtypeStruct((n, d), F32),
            jax.ShapeDtypeStruct((n, V7X_LANES), jnp.int32),
            jax.ShapeDtypeStruct((n, V7X_LANES), F32),
            jax.ShapeDtypeStruct((V7X_SUBLANES, V7X_LANES), F32),
        ],
        scratch_shapes=[pltpu.VMEM((tm + 2 * sub, conv_w), F32), pltpu.VMEM(w_out.shape, w_out.dtype),
                        pltpu.SemaphoreType.DMA(()), pltpu.VMEM((tm, d), BF16)],
        compiler_params=_params(1),
        name="out_proj",
    )(xp, xs, gb, u, u, u, attn_p, attn_s, conv_w_, conv_b, w_out, g2, rw, rb)


def _dispatch_kernel(pos_ref, h_hbm, xs_hbm, buf, sem_in, sem_out, *, tm, n_tiles):
    i = pl.program_id(0)
    slot = i % 2

    def load(tile, s):
        return pltpu.make_async_copy(h_hbm.at[pl.ds(tile * tm, tm)], buf.at[s], sem_in.at[s])

    def row_out(s, t, dst_row):
        return pltpu.make_async_copy(buf.at[s, pl.ds(t, 1)], xs_hbm.at[pl.ds(dst_row, 1)], sem_out.at[s])

    def drain(s):
        for _ in range(TOP_K):
            pltpu.make_async_copy(buf.at[s], xs_hbm.at[pl.ds(0, tm)], sem_out.at[s]).wait()

    @pl.when(i == 0)
    def _():
        load(0, 0).start()

    @pl.when(i > 0)
    def _():
        drain(1 - slot)

    @pl.when(i + 1 < n_tiles)
    def _():
        load(i + 1, 1 - slot).start()

    load(i, slot).wait()

    def issue(t, carry):
        for kk in range(TOP_K):
            row_out(slot, t, pos_ref[0, 0, t * TOP_K + kk]).start(priority=kk % 2)
        return carry

    lax.fori_loop(0, tm, issue, 0, unroll=8)

    @pl.when(i == n_tiles - 1)
    def _():
        drain(slot)


def _dispatch(pos, h2):
    n, d = h2.shape
    tm = _pick(256, n)
    n_tiles = n // tm
    pos3 = pos.reshape(n_tiles, 1, tm * TOP_K)
    kernel = functools.partial(_dispatch_kernel, tm=tm, n_tiles=n_tiles)
    return pl.pallas_call(
        kernel,
        grid=(n_tiles,),
        in_specs=[
            pl.BlockSpec((1, 1, tm * TOP_K), lambda i: (i, 0, 0), memory_space=pltpu.SMEM),
            pl.BlockSpec(memory_space=pl.ANY),
        ],
        out_specs=pl.BlockSpec(memory_space=pl.ANY),
        out_shape=jax.ShapeDtypeStruct((n * TOP_K, d), h2.dtype),
        scratch_shapes=[pltpu.VMEM((2, tm, d), h2.dtype), pltpu.SemaphoreType.DMA((2,)),
                        pltpu.SemaphoreType.DMA((2,))],
        compiler_params=_params(1),
        name="dispatch",
    )(pos3, h2)


def _experts_kernel(vt_ref, ve_ref, vfirst_ref, vlo_ref, vhi_ref,
                    x_ref, wg_ref, wu_ref, bg_ref, bu_ref, wd_ref, bd_ref, o_ref, act_ref, *, nf, tf):
    v = pl.program_id(0)
    f = pl.program_id(1)
    lo = vlo_ref[v]
    hi = vhi_ref[v]

    @pl.when(hi > lo)
    def _():
        xb = x_ref[...].astype(BF16)
        gate = jnp.dot(xb, wg_ref[0], preferred_element_type=F32) + bg_ref[0]
        up = jnp.dot(xb, wu_ref[0], preferred_element_type=F32) + bu_ref[0]
        gate = jnp.minimum(gate, SWIGLU_LIMIT)
        up = jnp.clip(up, -SWIGLU_LIMIT, SWIGLU_LIMIT)
        glu = gate * jax.nn.sigmoid(gate * SWIGLU_ALPHA)
        rows = lax.broadcasted_iota(jnp.int32, (xb.shape[0], 1), 0)
        mine = (rows >= lo) & (rows < hi)
        act_ref[f] = jnp.where(mine, (up + 1.0) * glu, 0.0).astype(BF16)

        def down():
            y = jnp.dot(act_ref[0], wd_ref[0, 0:tf, :], preferred_element_type=F32)
            for j in range(1, nf):
                y = y + jnp.dot(act_ref[j], wd_ref[0, j * tf:(j + 1) * tf, :], preferred_element_type=F32)
            return y + jnp.where(mine, bd_ref[0], 0.0)

        last = f == nf - 1
        fresh = vfirst_ref[v] == 1

        @pl.when(jnp.logical_and(last, fresh))
        def _():
            o_ref[...] = down()

        @pl.when(jnp.logical_and(last, jnp.logical_not(fresh)))
        def _():
            o_ref[...] = o_ref[...] + down()


def _visit_plan(counts, n_rows, tm, n_visits):
    n_exp = counts.shape[0]
    ends = jnp.cumsum(counts)
    starts = ends - counts
    first_tile = starts // tm
    last_tile = jnp.maximum(ends - 1, 0) // tm
    nvis = jnp.where(counts > 0, last_tile - first_tile + 1, 0)
    vis_end = jnp.cumsum(nvis)
    vis_start = vis_end - nvis
    total = vis_end[-1]
    v = jnp.arange(n_visits, dtype=jnp.int32)
    e = jnp.sum((v[:, None] >= vis_end[None, :]).astype(jnp.int32), axis=1)
    e = jnp.minimum(e, n_exp - 1)
    valid = v < total
    onehot = (e[:, None] == jnp.arange(n_exp, dtype=jnp.int32)[None, :]).astype(jnp.int32)
    pickv = lambda a: jnp.sum(onehot * a[None, :], axis=1)
    tile = pickv(first_tile) + (v - pickv(vis_start))
    lo = jnp.maximum(pickv(starts), tile * tm) - tile * tm
    hi = jnp.minimum(pickv(ends), (tile + 1) * tm) - tile * tm
    last_e = jnp.sum((total - 1 >= vis_end).astype(jnp.int32))
    tile = jnp.where(valid, tile, n_rows // tm - 1)
    e = jnp.where(valid, e, jnp.minimum(last_e, n_exp - 1))
    lo = jnp.where(valid, lo, 0)
    hi = jnp.where(valid, hi, 0)
    first = jnp.where(valid & (lo == 0), 1, 0)
    i32 = lambda a: a.astype(jnp.int32)
    return i32(tile), i32(e), i32(first), i32(lo), i32(hi)


def _experts(plan, xs, w_gu, b_gu, w_d, b_d):
    n_rows, d = xs.shape
    n_exp, _, two_f = w_gu.shape
    d_ff = two_f // 2
    tm = _pick(512, n_rows)
    tf = _pick(1024, d_ff)
    nf = d_ff // tf
    n_visits = plan[0].shape[0]

    def fx(f, hi, lo, v):
        return jnp.where(hi[v] > lo[v], f, 0)

    grid_spec = pltpu.PrefetchScalarGridSpec(
        num_scalar_prefetch=5,
        grid=(n_visits, nf),
        in_specs=[
            pl.BlockSpec((tm, d), lambda v, f, vt, ve, vf, lo, hi: (vt[v], 0)),
            pl.BlockSpec((1, d, tf), lambda v, f, vt, ve, vf, lo, hi: (ve[v], 0, fx(f, hi, lo, v))),
            pl.BlockSpec((1, d, tf), lambda v, f, vt, ve, vf, lo, hi: (ve[v], 0, nf + fx(f, hi, lo, v))),
            pl.BlockSpec((1, 1, tf), lambda v, f, vt, ve, vf, lo, hi: (ve[v], 0, fx(f, hi, lo, v))),
            pl.BlockSpec((1, 1, tf), lambda v, f, vt, ve, vf, lo, hi: (ve[v], 0, nf + fx(f, hi, lo, v))),
            pl.BlockSpec((1, d_ff, d), lambda v, f, vt, ve, vf, lo, hi: (ve[v], 0, 0)),
            pl.BlockSpec((1, 1, d), lambda v, f, vt, ve, vf, lo, hi: (ve[v], 0, 0)),
        ],
        out_specs=pl.BlockSpec((tm, d), lambda v, f, vt, ve, vf, lo, hi: (vt[v], 0)),
        scratch_shapes=[pltpu.VMEM((nf, tm, tf), BF16)],
    )
    return pl.pallas_call(
        functools.partial(_experts_kernel, nf=nf, tf=tf),
        grid_spec=grid_spec,
        out_shape=jax.ShapeDtypeStruct((n_rows, d), F32),
        compiler_params=_params(2, vmem=V7X_VMEM_BYTES - 4 * 1024 * 1024),
        name="experts",
    )(*plan, xs, w_gu, w_gu, b_gu, b_gu, w_d, b_d)


def _combine_kernel(pos_ref, nxt_ref, y_hbm, x1_ref, w_ref, fg_ref, o_ref, buf, sem, *, tm, n_steps):
    i = pl.program_id(0)
    slot = i % 2

    def fetch(p_ref, s):
        def body(t, carry):
            for kk in range(TOP_K):
                pltpu.make_async_copy(y_hbm.at[pl.ds(p_ref[0, 0, t * TOP_K + kk], 1)],
                                      buf.at[s, kk, pl.ds(t, 1)], sem.at[s]).start(priority=kk % 2)
            return carry
        lax.fori_loop(0, tm, body, 0, unroll=8)

    @pl.when(i == 0)
    def _():
        fetch(pos_ref, 0)

    @pl.when(i + 1 < n_steps)
    def _():
        fetch(nxt_ref, 1 - slot)

    for kk in range(TOP_K):
        pltpu.make_async_copy(y_hbm.at[pl.ds(0, tm)], buf.at[slot, kk], sem.at[slot]).wait()
    acc = x1_ref[...]
    for kk in range(TOP_K):
        acc = acc + w_ref[:, kk:kk + 1] * buf[slot, kk]
    o_ref[...] = _rms(acc, fg_ref[...])


def _combine(pos, y_sorted, x1, wts, fg, *, row_start, rows):
    n, d = x1.shape
    tm = _pick(128, rows)
    n_steps = rows // tm
    t0 = row_start // tm
    pos3 = pos.reshape(n // tm, 1, tm * TOP_K)
    kernel = functools.partial(_combine_kernel, tm=tm, n_steps=n_steps)
    smem = lambda im: pl.BlockSpec((1, 1, tm * TOP_K), im, memory_space=pltpu.SMEM)
    return pl.pallas_call(
        kernel,
        grid=(n_steps,),
        in_specs=[
            smem(lambda i: (t0 + i, 0, 0)),
            smem(lambda i: (t0 + jnp.minimum(i + 1, n_steps - 1), 0, 0)),
            pl.BlockSpec(memory_space=pl.ANY),
            pl.BlockSpec((tm, d), lambda i: (t0 + i, 0)),
            pl.BlockSpec((tm, V7X_LANES), lambda i: (t0 + i, 0)),
            _resident((1, d)),
        ],
        out_specs=pl.BlockSpec((tm, d), lambda i: (i, 0)),
        out_shape=jax.ShapeDtypeStruct((rows, d), F32),
        scratch_shapes=[pltpu.VMEM((2, TOP_K, tm, d), F32), pltpu.SemaphoreType.DMA((2,))],
        compiler_params=_params(1),
        name="combine",
    )(pos3, pos3, y_sorted, x1, wts, fg)


def _rope_tables(n_tokens):
    rows = n_tokens // GRID_W
    axis_dim = HEAD_DIM // 2
    row = jnp.repeat(jnp.arange(rows, dtype=F32), GRID_W)
    col = jnp.tile(jnp.arange(GRID_W, dtype=F32), rows)
    inv_freq = ROPE_THETA ** (-jnp.arange(0, axis_dim, 2, dtype=F32) / axis_dim)
    ang_r = row[:, None] * inv_freq[None, :]
    ang_c = col[:, None] * inv_freq[None, :]
    ang = jnp.concatenate([ang_r, ang_r, ang_c, ang_c], axis=-1)
    quarter = jnp.arange(HEAD_DIM) % (HEAD_DIM // 2) < HEAD_DIM // 4
    return jnp.cos(ang), jnp.where(quarter[None, :], -1.0, 1.0).astype(F32) * jnp.sin(ang)


def kernel(x_prompt, x_sample, norm1_g, w_in, conv_w, conv_b, q_norm_g, k_norm_g, w_out, norm2_g,
           router_w, router_b, w_gate_up, b_gate_up, w_down, b_down, final_g):
    assert norm1_g.shape[0] == 1, "one layer"
    b_p, s_p, d = x_prompt.shape
    b_s, s_s, _ = x_sample.shape
    n_p, n_s = b_p * s_p, b_s * s_s
    n = n_p + n_s
    n_exp = router_w.shape[-1]
    assert n_exp <= V7X_LANES and n_p % s_s == 0
    kchunk = _pick(1024, math.gcd(s_p, s_s))
    xp = x_prompt.reshape(n_p, d)
    xs = x_sample.reshape(n_s, d)
    cos, sin_signed = _rope_tables(max(s_p, s_s))

    gb, u, qT, k, vT = _in_proj(
        xp, xs, norm1_g[0][None, :], w_in[0].astype(BF16), q_norm_g[0][None, :], k_norm_g[0][None, :],
        cos, sin_signed, s_p=s_p, s_s=s_s, kchunk=kchunk)
    wgu, wd = w_gate_up[0], w_down[0]
    attn_p, wgu_b = _attention(qT, k, vT, wgu.reshape(-1, wgu.shape[-1]), row_start=0, batch=b_p, seq=s_p,
                               kchunk=kchunk)
    attn_s, wd_b = _attention(qT, k, vT, wd.reshape(-1, wd.shape[-1]), row_start=n_p, batch=b_s, seq=s_s,
                              kchunk=kchunk)

    rw = jnp.zeros((d, V7X_LANES), BF16).at[:, :n_exp].set(router_w[0].astype(BF16))
    rb = jnp.full((1, V7X_LANES), NEG_PAD, F32).at[0, :n_exp].set(router_b[0])
    x1, h2, meta_i, meta_w, cnt = _out_proj(
        xp, xs, gb, u, attn_p, attn_s, conv_w[0], conv_b[0][None, :], w_out[0].astype(BF16),
        norm2_g[0][None, :], rw, rb, s_p=s_p, s_s=s_s)

    counts = cnt[0, :n_exp].astype(jnp.int32)
    starts = jnp.cumsum(counts) - counts
    ids = meta_i[:, :TOP_K]
    ranks = meta_i[:, TOP_K:2 * TOP_K]
    onehot = ids[:, :, None] == jnp.arange(n_exp, dtype=jnp.int32)[None, None, :]
    pos = ranks + jnp.sum(jnp.where(onehot, starts[None, None, :], 0), axis=-1)

    x_sorted = _dispatch(pos, h2)
    n_rows = n * TOP_K
    tm_e = _pick(512, n_rows)
    plan = _visit_plan(counts, n_rows, tm_e, n_rows // tm_e + n_exp - 1)
    y_sorted = _experts(plan, x_sorted, wgu_b.reshape(wgu.shape), b_gate_up[0][:, None, :],
                        wd_b.reshape(wd.shape), b_down[0][:, None, :])

    fg = final_g[None, :]
    y_p = _combine(pos, y_sorted, x1, meta_w, fg, row_start=0, rows=n_p)
    y_s = _combine(pos, y_sorted, x1, meta_w, fg, row_start=n_p, rows=n_s)
    return (y_p.reshape(b_p, s_p, d), y_s.reshape(b_s, s_s, d))
```

```python
import functools
import math

import jax
import jax.numpy as jnp
from jax import lax
from jax.experimental import pallas as pl
from jax.experimental.pallas import tpu as pltpu

HEAD_DIM = 128
N_KV_HEADS = 2
GRID_W = 64
ROPE_THETA = 10000.0
TOP_K = 4
SWIGLU_LIMIT = 7.0
SWIGLU_ALPHA = 1.702
NORM_EPS = 1e-6

V7X_LANES = 128
V7X_SUBLANES = 8
V7X_VMEM_BYTES = 64 * 1024 * 1024
VMEM_LIMIT = V7X_VMEM_BYTES - 8 * 1024 * 1024
NEG_PAD = -1e30
OUT_COL_CHUNKS = 8

F32 = jnp.float32
BF16 = jnp.bfloat16
LOG2E = math.log2(math.e)


def _pick(target, n):
    t = min(target, n)
    while n % t:
        t //= 2
    return t


def _params(n_axes, vmem=VMEM_LIMIT):
    return pltpu.CompilerParams(dimension_semantics=("arbitrary",) * n_axes, vmem_limit_bytes=vmem)


def _resident(shape):
    nd = len(shape)
    return pl.BlockSpec(shape, lambda *_: (0,) * nd)


def _load_once(src_hbm, dst_vmem, sem):
    @pl.when(pl.program_id(0) == 0)
    def _():
        copy = pltpu.make_async_copy(src_hbm, dst_vmem, sem)
        copy.start()
        copy.wait()


def _rms(x, g):
    ms = jnp.mean(x * x, axis=-1, keepdims=True)
    return x * lax.rsqrt(ms + NORM_EPS) * g


def _in_proj_kernel(xp_ref, xs_ref, g1_ref, w_hbm, qg_ref, kg_ref, cos_ref, sin_ref,
                    gb_ref, u_ref, qT_ref, k_ref, vT_ref, w_ref, w_sem, *, n_prompt_tiles, conv_w, n_q, q_scale):
    _load_once(w_hbm, w_ref, w_sem)
    i = pl.program_id(0)
    x = jnp.where(i < n_prompt_tiles, xp_ref[...], xs_ref[...])
    h = _rms(x, g1_ref[...]).astype(BF16)

    def proj(c0, c1):
        return jnp.dot(h, w_ref[:, c0:c1], preferred_element_type=F32)

    cos = cos_ref[...]
    sin = sin_ref[...]
    lane = lax.broadcasted_iota(jnp.int32, (1, HEAD_DIM), 1)
    first_half = (lane % (HEAD_DIM // 2)) < (HEAD_DIM // 4)

    def norm_rope(xh, g):
        y = _rms(xh, g)
        up = pltpu.roll(y, HEAD_DIM - HEAD_DIM // 4, 1)
        dn = pltpu.roll(y, HEAD_DIM // 4, 1)
        return y * cos + jnp.where(first_half, up, dn) * sin

    q0 = 3 * conv_w
    k0 = q0 + n_q * HEAD_DIM
    q = proj(q0, k0)
    kv = proj(k0, k0 + 2 * N_KV_HEADS * HEAD_DIM)

    def q_head(hd):
        qh = norm_rope(q[:, hd * HEAD_DIM:(hd + 1) * HEAD_DIM], qg_ref[...]) * q_scale
        qT_ref[hd] = qh.T.astype(BF16)

    def kv_head(hd):
        kh = norm_rope(kv[:, hd * HEAD_DIM:(hd + 1) * HEAD_DIM], kg_ref[...])
        k_ref[:, hd * HEAD_DIM:(hd + 1) * HEAD_DIM] = kh.astype(BF16)
        vh = kv[:, (N_KV_HEADS + hd) * HEAD_DIM:(N_KV_HEADS + hd + 1) * HEAD_DIM]
        vT_ref[hd, 0] = vh.T.astype(BF16)

    head_work = ([functools.partial(q_head, hd) for hd in range(n_q)]
                 + [functools.partial(kv_head, hd) for hd in range(N_KV_HEADS)])
    cw = HEAD_DIM * 2
    n_cc = conv_w // cw
    for c in range(n_cc):
        cols = slice(c * cw, (c + 1) * cw)
        gb_ref[:, cols] = proj(c * cw, (c + 1) * cw)
        if 2 * c < len(head_work):
            head_work[2 * c]()
        u_ref[:, cols] = proj(conv_w + c * cw, conv_w + (c + 1) * cw) * proj(2 * conv_w + c * cw,
                                                                            2 * conv_w + (c + 1) * cw)
        if 2 * c + 1 < len(head_work):
            head_work[2 * c + 1]()
    for work in head_work[2 * n_cc:]:
        work()


def _in_proj(xp, xs, g1, w_in, qg, kg, cos, sin_signed, *, s_p, s_s, kchunk):
    n_p, d = xp.shape
    n_s = xs.shape[0]
    n = n_p + n_s
    conv_w = d // 2
    n_q = (d - conv_w) // HEAD_DIM
    tm = _pick(256, math.gcd(s_p, s_s))
    npt, spt, sst = n_p // tm, s_p // tm, s_s // tm
    r = kchunk // tm
    q_scale = LOG2E / math.sqrt(HEAD_DIM)

    def pos_map(i):
        return (jnp.where(i < npt, i % spt, (i - npt) % sst), 0)

    kernel = functools.partial(_in_proj_kernel, n_prompt_tiles=npt, conv_w=conv_w, n_q=n_q, q_scale=q_scale)
    return pl.pallas_call(
        kernel,
        grid=(n // tm,),
        in_specs=[
            pl.BlockSpec((tm, d), lambda i: (jnp.minimum(i, npt - 1), 0)),
            pl.BlockSpec((tm, d), lambda i: (jnp.maximum(i - npt, 0), 0)),
            _resident((1, d)),
            pl.BlockSpec(memory_space=pl.ANY),
            _resident((1, HEAD_DIM)),
            _resident((1, HEAD_DIM)),
            pl.BlockSpec((tm, HEAD_DIM), pos_map),
            pl.BlockSpec((tm, HEAD_DIM), pos_map),
        ],
        out_specs=[
            pl.BlockSpec((tm, conv_w), lambda i: (i, 0)),
            pl.BlockSpec((tm, conv_w), lambda i: (i, 0)),
            pl.BlockSpec((n_q, HEAD_DIM, tm), lambda i: (0, 0, i)),
            pl.BlockSpec((tm, N_KV_HEADS * HEAD_DIM), lambda i: (i, 0)),
            pl.BlockSpec((N_KV_HEADS, 1, HEAD_DIM, tm), lambda i: (0, i // r, 0, i % r)),
        ],
        out_shape=[
            jax.ShapeDtypeStruct((n, conv_w), F32),
            jax.ShapeDtypeStruct((n, conv_w), F32),
            jax.ShapeDtypeStruct((n_q, HEAD_DIM, n), BF16),
            jax.ShapeDtypeStruct((n, N_KV_HEADS * HEAD_DIM), BF16),
            jax.ShapeDtypeStruct((N_KV_HEADS, n // kchunk, HEAD_DIM, kchunk), BF16),
        ],
        scratch_shapes=[pltpu.VMEM(w_in.shape, w_in.dtype), pltpu.SemaphoreType.DMA(())],
        compiler_params=_params(1),
        name="in_proj",
    )(xp, xs, g1, w_in, qg, kg, cos, sin_signed)


def _attention_kernel(qT_ref, k_ref, vT_ref, w_ref, o_ref, wb_ref, m_ref, l_ref, acc_ref, s_ref, mx_ref,
                      *, n_chunks, kchunk, ksub, group):
    wb_ref[...] = w_ref[...].astype(BF16)

    m_ref[...] = jnp.full(m_ref.shape, -jnp.inf, F32)
    l_ref[...] = jnp.zeros(l_ref.shape, F32)
    acc_ref[...] = jnp.zeros(acc_ref.shape, F32)
    sub = V7X_SUBLANES

    n_sub = kchunk // ksub

    def scores_sub(c, g, par, j, mx):
        r0 = pl.multiple_of(c * kchunk + j * ksub, ksub)
        sT = jnp.dot(k_ref[pl.ds(r0, ksub), :], qT_ref[g], preferred_element_type=F32)
        s_ref[par, j * ksub:(j + 1) * ksub, :] = sT
        cm = jnp.max(sT.reshape(ksub // sub, sub, sT.shape[1]), axis=0)
        return cm if mx is None else jnp.maximum(mx, cm)

    def consume_sub(c, par, j, m_new, pv, l):
        p = jnp.exp2(s_ref[par, j * ksub:(j + 1) * ksub, :] - m_new)
        d = jnp.dot(vT_ref[0, c, :, j * ksub:(j + 1) * ksub], p.astype(BF16), preferred_element_type=F32)
        ps = jnp.sum(p.reshape(ksub // sub, sub, p.shape[1]), axis=0)
        return (d if pv is None else pv + d), (ps if l is None else l + ps)

    def pair(c_next, g_next, par_next, c, g, par):
        m_old = m_ref[g]
        m_new = jnp.maximum(m_old, jnp.max(mx_ref[par], axis=0, keepdims=True))
        alpha = jnp.exp2(m_old - m_new)
        mx = pv = l = None
        for j in range(n_sub):
            mx = scores_sub(c_next, g_next, par_next, j, mx)
            pv, l = consume_sub(c, par, j, m_new, pv, l)
        mx_ref[par_next] = mx
        l_ref[g] = alpha * l_ref[g] + jnp.sum(l, axis=0, keepdims=True)
        acc_ref[g] = alpha * acc_ref[g] + pv
        m_ref[g] = m_new

    def parity(c, g):
        return (g % 2) if group % 2 == 0 else (c * group + g) % 2

    mx0 = None
    for j in range(n_sub):
        mx0 = scores_sub(0, 0, 0, j, mx0)
    mx_ref[0] = mx0

    def chunk(c, carry):
        for g in range(group):
            if g + 1 < group:
                pair(c, g + 1, parity(c, g + 1), c, g, parity(c, g))
            else:
                pair(jnp.minimum(c + 1, n_chunks - 1), 0, parity(c + 1, 0), c, g, parity(c, g))
        return carry

    lax.fori_loop(0, n_chunks, chunk, 0, unroll=math.gcd(n_chunks, 8))
    for g in range(group):
        o = acc_ref[g] / l_ref[g]
        o_ref[:, g * HEAD_DIM:(g + 1) * HEAD_DIM] = o.T.astype(o_ref.dtype)


def _attention(qT, k, vT, w, *, row_start, batch, seq, kchunk):
    n_q = qT.shape[0]
    group = n_q // N_KV_HEADS
    tq = _pick(256, seq)
    nqt = seq // tq
    q0 = row_start // tq
    s0 = row_start // seq
    n_chunks = seq // kchunk
    n_steps = batch * N_KV_HEADS * nqt
    w_rows, w_cols = w.shape
    wr = w_rows // n_steps
    assert wr * n_steps == w_rows and wr % (2 * V7X_SUBLANES) == 0, "weight rows must split evenly over the grid"
    step = lambda b, h, i: ((b * N_KV_HEADS + h) * nqt + i, 0)
    kernel = functools.partial(_attention_kernel, n_chunks=n_chunks, kchunk=kchunk, ksub=_pick(256, kchunk),
                               group=group)
    return pl.pallas_call(
        kernel,
        grid=(batch, N_KV_HEADS, nqt),
        in_specs=[
            pl.BlockSpec((group, HEAD_DIM, tq), lambda b, h, i: (h, 0, q0 + b * nqt + i)),
            pl.BlockSpec((seq, HEAD_DIM), lambda b, h, i: (s0 + b, h)),
            pl.BlockSpec((1, n_chunks, HEAD_DIM, kchunk), lambda b, h, i: (h, s0 + b, 0, 0)),
            pl.BlockSpec((wr, w_cols), step),
        ],
        out_specs=[
            pl.BlockSpec((tq, group * HEAD_DIM), lambda b, h, i: (b * nqt + i, h)),
            pl.BlockSpec((wr, w_cols), step),
        ],
        out_shape=[
            jax.ShapeDtypeStruct((batch * seq, n_q * HEAD_DIM), BF16),
            jax.ShapeDtypeStruct((w_rows, w_cols), BF16),
        ],
        scratch_shapes=[
            pltpu.VMEM((group, 1, tq), F32),
            pltpu.VMEM((group, 1, tq), F32),
            pltpu.VMEM((group, HEAD_DIM, tq), F32),
            pltpu.VMEM((2, kchunk, tq), F32),
            pltpu.VMEM((2, V7X_SUBLANES, tq), F32),
        ],
        compiler_params=_params(3),
        name="attention",
    )(qT, k, vT, w)


def _out_proj_kernel(xp_ref, xs_ref, gb_ref, u_ref, up_ref, un_ref, ap_ref, as_ref, cw_ref, cb_ref, wo_hbm,
                     g2_ref, rw_ref, rb_ref,
                     x1_ref, h2_ref, mi_ref, mw_ref, cnt_ref, ubuf, wo_ref, wo_sem, hprev,
                     *, n_prompt_tiles, n_tiles, spt, sst, conv_w, tm):
    _load_once(wo_hbm, wo_ref, wo_sem)
    step = pl.program_id(0)

    @pl.when(step == 0)
    def _():
        cnt_ref[...] = jnp.zeros(cnt_ref.shape, F32)
        hprev[...] = jnp.zeros(hprev.shape, hprev.dtype)

    routed = hprev[...]
    i = jnp.minimum(step, n_tiles - 1)
    is_p = i < n_prompt_tiles
    local = jnp.where(is_p, i % spt, (i - n_prompt_tiles) % sst)
    per_seq = jnp.where(is_p, spt, sst)
    keep_prev = (local > 0).astype(F32)
    keep_next = (local < per_seq - 1).astype(F32)

    sub = V7X_SUBLANES
    ubuf[0:sub, :] = up_ref[...] * keep_prev
    ubuf[sub:sub + tm, :] = u_ref[...]
    ubuf[sub + tm:2 * sub + tm, :] = un_ref[...] * keep_next
    conv = (ubuf[sub - 1:sub - 1 + tm, :] * cw_ref[0:1, :] + ubuf[sub:sub + tm, :] * cw_ref[1:2, :]
            + ubuf[sub + 1:sub + 1 + tm, :] * cw_ref[2:3, :] + cb_ref[...])
    conv_out = (gb_ref[...] * conv).astype(BF16)
    attn = jnp.where(is_p, ap_ref[...], as_ref[...])
    logits = jnp.dot(routed, rw_ref[...], preferred_element_type=F32) + rb_ref[...]
    lane = lax.broadcasted_iota(jnp.int32, logits.shape, 1)
    st = dict(work=logits, vals=[], hots=[])

    def select_round():
        work = st["work"]
        m = jnp.max(work, axis=-1, keepdims=True)
        idx = jnp.min(jnp.where(work == m, lane, V7X_LANES), axis=-1, keepdims=True)
        hot = lane == idx
        st["vals"].append(m)
        st["hots"].append((hot, idx))
        st["work"] = jnp.where(hot, -jnp.inf, work)

    def count_before():
        sel = jnp.zeros(logits.shape, F32)
        for hot, _ in st["hots"]:
            sel = sel + hot.astype(F32)
        sel = sel * (step > 0).astype(F32)
        row = lax.broadcasted_iota(jnp.int32, (tm, tm), 0)
        col = lax.broadcasted_iota(jnp.int32, (tm, tm), 1)
        lower = jnp.where(col < row, 1.0, 0.0).astype(BF16)
        st["before"] = jnp.dot(lower, sel.astype(BF16), preferred_element_type=F32) + cnt_ref[0:1, :]
        cnt_ref[...] = cnt_ref[...] + jnp.sum(sel, axis=0, keepdims=True)

    def write_meta():
        vals = st["vals"]
        es = [jnp.exp(v - vals[0]) for v in vals]
        den = es[0] + es[1] + es[2] + es[3]
        mi = jnp.zeros(logits.shape, jnp.int32)
        mw = jnp.zeros(logits.shape, F32)
        for kk, (hot, idx) in enumerate(st["hots"]):
            rank = jnp.sum(jnp.where(hot, st["before"], 0.0), axis=-1, keepdims=True).astype(jnp.int32)
            mi = jnp.where(lane == kk, idx, mi)
            mi = jnp.where(lane == TOP_K + kk, rank, mi)
            mw = jnp.where(lane == kk, es[kk] / den, mw)
        mi_ref[...] = mi
        mw_ref[...] = mw

    router_stages = [select_round] * TOP_K + [count_before, write_meta]
    d_out = wo_ref.shape[1]
    cw = d_out // OUT_COL_CHUNKS
    ssq = None
    for c in range(OUT_COL_CHUNKS):
        cols = slice(c * cw, (c + 1) * cw)
        x_c = jnp.where(is_p, xp_ref[:, cols], xs_ref[:, cols])
        x1_c = (x_c + jnp.dot(conv_out, wo_ref[0:conv_w, cols], preferred_element_type=F32)
                + jnp.dot(attn, wo_ref[conv_w:, cols], preferred_element_type=F32))
        x1_ref[:, cols] = x1_c
        part = jnp.sum(x1_c * x1_c, axis=-1, keepdims=True)
        ssq = part if ssq is None else ssq + part
        if c < len(router_stages):
            router_stages[c]()
    for stage in router_stages[OUT_COL_CHUNKS:]:
        stage()
    h2 = x1_ref[...] * lax.rsqrt(ssq / d_out + NORM_EPS) * g2_ref[...]
    h2_ref[...] = h2
    hprev[...] = h2.astype(BF16)


def _out_proj(xp, xs, gb, u, attn_p, attn_s, conv_w_, conv_b, w_out, g2, rw, rb, *, s_p, s_s):
    n_p, d = xp.shape
    n_s = xs.shape[0]
    n = n_p + n_s
    conv_w = gb.shape[1]
    a_w = attn_p.shape[1]
    tm = _pick(256, math.gcd(s_p, s_s))
    npt, spt, sst = n_p // tm, s_p // tm, s_s // tm
    sub = V7X_SUBLANES
    hb = tm // sub
    nt = n // tm
    kernel = functools.partial(_out_proj_kernel, n_prompt_tiles=npt, n_tiles=nt, spt=spt, sst=sst, conv_w=conv_w,
                               tm=tm)
    tile = lambda s: jnp.minimum(s, nt - 1)
    row = lambda s: (tile(s), 0)
    routed_row = lambda s: (jnp.maximum(s - 1, 0), 0)
    return pl.pallas_call(
        kernel,
        grid=(nt + 1,),
        in_specs=[
            pl.BlockSpec((tm, d), lambda s: (jnp.minimum(tile(s), npt - 1), 0)),
            pl.BlockSpec((tm, d), lambda s: (jnp.maximum(tile(s) - npt, 0), 0)),
            pl.BlockSpec((tm, conv_w), row),
            pl.BlockSpec((tm, conv_w), row),
            pl.BlockSpec((sub, conv_w), lambda s: (jnp.maximum(tile(s) * hb - 1, 0), 0)),
            pl.BlockSpec((sub, conv_w), lambda s: (jnp.minimum((tile(s) + 1) * hb, n // sub - 1), 0)),
            pl.BlockSpec((tm, a_w), lambda s: (jnp.minimum(tile(s), npt - 1), 0)),
            pl.BlockSpec((tm, a_w), lambda s: (jnp.maximum(tile(s) - npt, 0), 0)),
            _resident(conv_w_.shape),
            _resident((1, conv_w)),
            pl.BlockSpec(memory_space=pl.ANY),
            _resident((1, d)),
            _resident(rw.shape),
            _resident(rb.shape),
        ],
        out_specs=[
            pl.BlockSpec((tm, d), row),
            pl.BlockSpec((tm, d), row),
            pl.BlockSpec((tm, V7X_LANES), routed_row),
            pl.BlockSpec((tm, V7X_LANES), routed_row),
            pl.BlockSpec((V7X_SUBLANES, V7X_LANES), lambda s: (0, 0)),
        ],
        out_shape=[
            jax.ShapeDtypeStruct((n, d), F32),
            jax.ShapeDtypeStruct((n, d), F32),
            jax.ShapeDtypeStruct((n, V7X_LANES), jnp.int32),
            jax.ShapeDtypeStruct((n, V7X_LANES), F32),
            jax.ShapeDtypeStruct((V7X_SUBLANES, V7X_LANES), F32),
        ],
        scratch_shapes=[pltpu.VMEM((tm + 2 * sub, conv_w), F32), pltpu.VMEM(w_out.shape, w_out.dtype),
                        pltpu.SemaphoreType.DMA(()), pltpu.VMEM((tm, d), BF16)],
        compiler_params=_params(1),
        name="out_proj",
    )(xp, xs, gb, u, u, u, attn_p, attn_s, conv_w_, conv_b, w_out, g2, rw, rb)


def _dispatch_kernel(pos_ref, h_hbm, xs_hbm, buf, sem_in, sem_out, *, tm, n_tiles):
    i = pl.program_id(0)
    slot = i % 2

    def load(tile, s):
        return pltpu.make_async_copy(h_hbm.at[pl.ds(tile * tm, tm)], buf.at[s], sem_in.at[s])

    def row_out(s, t, dst_row):
        return pltpu.make_async_copy(buf.at[s, pl.ds(t, 1)], xs_hbm.at[pl.ds(dst_row, 1)], sem_out.at[s])

    def drain(s):
        for _ in range(TOP_K):
            pltpu.make_async_copy(buf.at[s], xs_hbm.at[pl.ds(0, tm)], sem_out.at[s]).wait()

    @pl.when(i == 0)
    def _():
        load(0, 0).start()

    @pl.when(i > 0)
    def _():
        drain(1 - slot)

    @pl.when(i + 1 < n_tiles)
    def _():
        load(i + 1, 1 - slot).start()

    load(i, slot).wait()

    def issue(t, carry):
        for kk in range(TOP_K):
            row_out(slot, t, pos_ref[0, 0, t * TOP_K + kk]).start(priority=kk % 2)
        return carry

    lax.fori_loop(0, tm, issue, 0, unroll=8)

    @pl.when(i == n_tiles - 1)
    def _():
        drain(slot)


def _dispatch(pos, h2):
    n, d = h2.shape
    tm = _pick(256, n)
    n_tiles = n // tm
    pos3 = pos.reshape(n_tiles, 1, tm * TOP_K)
    kernel = functools.partial(_dispatch_kernel, tm=tm, n_tiles=n_tiles)
    return pl.pallas_call(
        kernel,
        grid=(n_tiles,),
        in_specs=[
            pl.BlockSpec((1, 1, tm * TOP_K), lambda i: (i, 0, 0), memory_space=pltpu.SMEM),
            pl.BlockSpec(memory_space=pl.ANY),
        ],
        out_specs=pl.BlockSpec(memory_space=pl.ANY),
        out_shape=jax.ShapeDtypeStruct((n * TOP_K, d), h2.dtype),
        scratch_shapes=[pltpu.VMEM((2, tm, d), h2.dtype), pltpu.SemaphoreType.DMA((2,)),
                        pltpu.SemaphoreType.DMA((2,))],
        compiler_params=_params(1),
        name="dispatch",
    )(pos3, h2)


def _experts_kernel(vt_ref, ve_ref, vfirst_ref, vlo_ref, vhi_ref,
                    x_ref, wg_ref, wu_ref, bg_ref, bu_ref, wd_ref, bd_ref, o_ref, act_ref, *, nf, tf):
    v = pl.program_id(0)
    f = pl.program_id(1)
    lo = vlo_ref[v]
    hi = vhi_ref[v]
    tm = x_ref.shape[0]
    last = f == nf - 1
    fresh = vfirst_ref[v] == 1

    def run(r0, nr):
        rs = slice(r0, r0 + nr)
        xb = x_ref[rs, :].astype(BF16)
        gate = jnp.dot(xb, wg_ref[0], preferred_element_type=F32) + bg_ref[0]
        up = jnp.dot(xb, wu_ref[0], preferred_element_type=F32) + bu_ref[0]
        gate = jnp.minimum(gate, SWIGLU_LIMIT)
        up = jnp.clip(up, -SWIGLU_LIMIT, SWIGLU_LIMIT)
        glu = gate * jax.nn.sigmoid(gate * SWIGLU_ALPHA)
        rows = r0 + lax.broadcasted_iota(jnp.int32, (nr, 1), 0)
        mine = (rows >= lo) & (rows < hi)
        act_ref[f, rs, :] = jnp.where(mine, (up + 1.0) * glu, 0.0).astype(BF16)

        def down():
            y = jnp.dot(act_ref[0, rs, :], wd_ref[0, 0:tf, :], preferred_element_type=F32)
            for j in range(1, nf):
                y = y + jnp.dot(act_ref[j, rs, :], wd_ref[0, j * tf:(j + 1) * tf, :], preferred_element_type=F32)
            return y + jnp.where(mine, bd_ref[0], 0.0)

        @pl.when(jnp.logical_and(last, fresh))
        def _():
            o_ref[rs, :] = down()

        @pl.when(jnp.logical_and(last, jnp.logical_not(fresh)))
        def _():
            o_ref[rs, :] = o_ref[rs, :] + down()

    whole = jnp.logical_and(lo == 0, hi == tm)

    @pl.when(whole)
    def _():
        run(0, tm)

    @pl.when(jnp.logical_and(hi > lo, jnp.logical_not(whole)))
    def _():
        half = tm // 2
        for r0 in (0, half):
            touched = jnp.logical_and(lo < r0 + half, hi > r0)

            @pl.when(touched)
            def _():
                run(r0, half)

            @pl.when(jnp.logical_and(jnp.logical_not(touched), jnp.logical_and(last, fresh)))
            def _():
                o_ref[r0:r0 + half, :] = jnp.zeros((half, o_ref.shape[1]), o_ref.dtype)


def _visit_plan(counts, n_rows, tm, n_visits):
    n_exp = counts.shape[0]
    ends = jnp.cumsum(counts)
    starts = ends - counts
    first_tile = starts // tm
    last_tile = jnp.maximum(ends - 1, 0) // tm
    nvis = jnp.where(counts > 0, last_tile - first_tile + 1, 0)
    vis_end = jnp.cumsum(nvis)
    vis_start = vis_end - nvis
    total = vis_end[-1]
    v = jnp.arange(n_visits, dtype=jnp.int32)
    e = jnp.sum((v[:, None] >= vis_end[None, :]).astype(jnp.int32), axis=1)
    e = jnp.minimum(e, n_exp - 1)
    valid = v < total
    onehot = (e[:, None] == jnp.arange(n_exp, dtype=jnp.int32)[None, :]).astype(jnp.int32)
    pickv = lambda a: jnp.sum(onehot * a[None, :], axis=1)
    tile = pickv(first_tile) + (v - pickv(vis_start))
    lo = jnp.maximum(pickv(starts), tile * tm) - tile * tm
    hi = jnp.minimum(pickv(ends), (tile + 1) * tm) - tile * tm
    last_e = jnp.sum((total - 1 >= vis_end).astype(jnp.int32))
    tile = jnp.where(valid, tile, n_rows // tm - 1)
    e = jnp.where(valid, e, jnp.minimum(last_e, n_exp - 1))
    lo = jnp.where(valid, lo, 0)
    hi = jnp.where(valid, hi, 0)
    first = jnp.where(valid & (lo == 0), 1, 0)
    i32 = lambda a: a.astype(jnp.int32)
    return i32(tile), i32(e), i32(first), i32(lo), i32(hi)


def _experts(plan, xs, w_gu, b_gu, w_d, b_d):
    n_rows, d = xs.shape
    n_exp, _, two_f = w_gu.shape
    d_ff = two_f // 2
    tm = _pick(512, n_rows)
    tf = _pick(1024, d_ff)
    nf = d_ff // tf
    n_visits = plan[0].shape[0]

    def fx(f, hi, lo, v):
        return jnp.where(hi[v] > lo[v], f, 0)

    grid_spec = pltpu.PrefetchScalarGridSpec(
        num_scalar_prefetch=5,
        grid=(n_visits, nf),
        in_specs=[
            pl.BlockSpec((tm, d), lambda v, f, vt, ve, vf, lo, hi: (vt[v], 0)),
            pl.BlockSpec((1, d, tf), lambda v, f, vt, ve, vf, lo, hi: (ve[v], 0, fx(f, hi, lo, v))),
            pl.BlockSpec((1, d, tf), lambda v, f, vt, ve, vf, lo, hi: (ve[v], 0, nf + fx(f, hi, lo, v))),
            pl.BlockSpec((1, 1, tf), lambda v, f, vt, ve, vf, lo, hi: (ve[v], 0, fx(f, hi, lo, v))),
            pl.BlockSpec((1, 1, tf), lambda v, f, vt, ve, vf, lo, hi: (ve[v], 0, nf + fx(f, hi, lo, v))),
            pl.BlockSpec((1, d_ff, d), lambda v, f, vt, ve, vf, lo, hi: (ve[v], 0, 0)),
            pl.BlockSpec((1, 1, d), lambda v, f, vt, ve, vf, lo, hi: (ve[v], 0, 0)),
        ],
        out_specs=pl.BlockSpec((tm, d), lambda v, f, vt, ve, vf, lo, hi: (vt[v], 0)),
        scratch_shapes=[pltpu.VMEM((nf, tm, tf), BF16)],
    )
    return pl.pallas_call(
        functools.partial(_experts_kernel, nf=nf, tf=tf),
        grid_spec=grid_spec,
        out_shape=jax.ShapeDtypeStruct((n_rows, d), F32),
        compiler_params=_params(2, vmem=V7X_VMEM_BYTES - 4 * 1024 * 1024),
        name="experts",
    )(*plan, xs, w_gu, w_gu, b_gu, b_gu, w_d, b_d)


def _combine_kernel(pos_ref, nxt_ref, y_hbm, x1_ref, w_ref, fg_ref, o_ref, buf, sem, *, tm, n_steps):
    i = pl.program_id(0)
    slot = i % 2

    def fetch(p_ref, s):
        def body(t, carry):
            for kk in range(TOP_K):
                pltpu.make_async_copy(y_hbm.at[pl.ds(p_ref[0, 0, t * TOP_K + kk], 1)],
                                      buf.at[s, kk, pl.ds(t, 1)], sem.at[s]).start(priority=kk % 2)
            return carry
        lax.fori_loop(0, tm, body, 0, unroll=8)

    @pl.when(i == 0)
    def _():
        fetch(pos_ref, 0)

    @pl.when(i + 1 < n_steps)
    def _():
        fetch(nxt_ref, 1 - slot)

    for kk in range(TOP_K):
        pltpu.make_async_copy(y_hbm.at[pl.ds(0, tm)], buf.at[slot, kk], sem.at[slot]).wait()
    acc = x1_ref[...]
    for kk in range(TOP_K):
        acc = acc + w_ref[:, kk:kk + 1] * buf[slot, kk]
    o_ref[...] = _rms(acc, fg_ref[...])


def _combine(pos, y_sorted, x1, wts, fg, *, row_start, rows):
    n, d = x1.shape
    tm = _pick(128, rows)
    n_steps = rows // tm
    t0 = row_start // tm
    pos3 = pos.reshape(n // tm, 1, tm * TOP_K)
    kernel = functools.partial(_combine_kernel, tm=tm, n_steps=n_steps)
    smem = lambda im: pl.BlockSpec((1, 1, tm * TOP_K), im, memory_space=pltpu.SMEM)
    return pl.pallas_call(
        kernel,
        grid=(n_steps,),
        in_specs=[
            smem(lambda i: (t0 + i, 0, 0)),
            smem(lambda i: (t0 + jnp.minimum(i + 1, n_steps - 1), 0, 0)),
            pl.BlockSpec(memory_space=pl.ANY),
            pl.BlockSpec((tm, d), lambda i: (t0 + i, 0)),
            pl.BlockSpec((tm, V7X_LANES), lambda i: (t0 + i, 0)),
            _resident((1, d)),
        ],
        out_specs=pl.BlockSpec((tm, d), lambda i: (i, 0)),
        out_shape=jax.ShapeDtypeStruct((rows, d), F32),
        scratch_shapes=[pltpu.VMEM((2, TOP_K, tm, d), F32), pltpu.SemaphoreType.DMA((2,))],
        compiler_params=_params(1),
        name="combine",
    )(pos3, pos3, y_sorted, x1, wts, fg)


def _rope_tables(n_tokens):
    rows = n_tokens // GRID_W
    axis_dim = HEAD_DIM // 2
    row = jnp.repeat(jnp.arange(rows, dtype=F32), GRID_W)
    col = jnp.tile(jnp.arange(GRID_W, dtype=F32), rows)
    inv_freq = ROPE_THETA ** (-jnp.arange(0, axis_dim, 2, dtype=F32) / axis_dim)
    ang_r = row[:, None] * inv_freq[None, :]
    ang_c = col[:, None] * inv_freq[None, :]
    ang = jnp.concatenate([ang_r, ang_r, ang_c, ang_c], axis=-1)
    quarter = jnp.arange(HEAD_DIM) % (HEAD_DIM // 2) < HEAD_DIM // 4
    return jnp.cos(ang), jnp.where(quarter[None, :], -1.0, 1.0).astype(F32) * jnp.sin(ang)


def kernel(x_prompt, x_sample, norm1_g, w_in, conv_w, conv_b, q_norm_g, k_norm_g, w_out, norm2_g,
           router_w, router_b, w_gate_up, b_gate_up, w_down, b_down, final_g):
    assert norm1_g.shape[0] == 1, "one layer"
    b_p, s_p, d = x_prompt.shape
    b_s, s_s, _ = x_sample.shape
    n_p, n_s = b_p * s_p, b_s * s_s
    n = n_p + n_s
    n_exp = router_w.shape[-1]
    assert n_exp <= V7X_LANES and n_p % s_s == 0
    kchunk = _pick(1024, math.gcd(s_p, s_s))
    xp = x_prompt.reshape(n_p, d)
    xs = x_sample.reshape(n_s, d)
    cos, sin_signed = _rope_tables(max(s_p, s_s))

    gb, u, qT, k, vT = _in_proj(
        xp, xs, norm1_g[0][None, :], w_in[0].astype(BF16), q_norm_g[0][None, :], k_norm_g[0][None, :],
        cos, sin_signed, s_p=s_p, s_s=s_s, kchunk=kchunk)
    wgu, wd = w_gate_up[0], w_down[0]
    attn_p, wgu_b = _attention(qT, k, vT, wgu.reshape(-1, wgu.shape[-1]), row_start=0, batch=b_p, seq=s_p,
                               kchunk=kchunk)
    attn_s, wd_b = _attention(qT, k, vT, wd.reshape(-1, wd.shape[-1]), row_start=n_p, batch=b_s, seq=s_s,
                              kchunk=kchunk)

    rw = jnp.zeros((d, V7X_LANES), BF16).at[:, :n_exp].set(router_w[0].astype(BF16))
    rb = jnp.full((1, V7X_LANES), NEG_PAD, F32).at[0, :n_exp].set(router_b[0])
    x1, h2, meta_i, meta_w, cnt = _out_proj(
        xp, xs, gb, u, attn_p, attn_s, conv_w[0], conv_b[0][None, :], w_out[0].astype(BF16),
        norm2_g[0][None, :], rw, rb, s_p=s_p, s_s=s_s)

    counts = cnt[0, :n_exp].astype(jnp.int32)
    starts = jnp.cumsum(counts) - counts
    ids = meta_i[:, :TOP_K]
    ranks = meta_i[:, TOP_K:2 * TOP_K]
    onehot = ids[:, :, None] == jnp.arange(n_exp, dtype=jnp.int32)[None, None, :]
    pos = ranks + jnp.sum(jnp.where(onehot, starts[None, None, :], 0), axis=-1)

    x_sorted = _dispatch(pos, h2)
    n_rows = n * TOP_K
    tm_e = _pick(512, n_rows)
    plan = _visit_plan(counts, n_rows, tm_e, n_rows // tm_e + n_exp - 1)
    y_sorted = _experts(plan, x_sorted, wgu_b.reshape(wgu.shape), b_gate_up[0][:, None, :],
                        wd_b.reshape(wd.shape), b_down[0][:, None, :])

    fg = final_g[None, :]
    y_p = _combine(pos, y_sorted, x1, meta_w, fg, row_start=0, rows=n_p)
    y_s = _combine(pos, y_sorted, x1, meta_w, fg, row_start=n_p, rows=n_s)
    return (y_p.reshape(b_p, s_p, d), y_s.reshape(b_s, s_s, d))
```

```python
import functools
import math

import jax
import jax.numpy as jnp
from jax import lax
from jax.experimental import pallas as pl
from jax.experimental.pallas import tpu as pltpu

HEAD_DIM = 128
N_KV_HEADS = 2
GRID_W = 64
ROPE_THETA = 10000.0
TOP_K = 4
SWIGLU_LIMIT = 7.0
SWIGLU_ALPHA = 1.702
NORM_EPS = 1e-6

V7X_LANES = 128
V7X_SUBLANES = 8
V7X_VMEM_BYTES = 64 * 1024 * 1024
VMEM_LIMIT = V7X_VMEM_BYTES - 8 * 1024 * 1024
NEG_PAD = -1e30
OUT_COL_CHUNKS = 8

F32 = jnp.float32
BF16 = jnp.bfloat16
LOG2E = math.log2(math.e)


def _pick(target, n):
    t = min(target, n)
    while n % t:
        t //= 2
    return t


def _params(n_axes, vmem=VMEM_LIMIT):
    return pltpu.CompilerParams(dimension_semantics=("arbitrary",) * n_axes, vmem_limit_bytes=vmem)


def _resident(shape):
    nd = len(shape)
    return pl.BlockSpec(shape, lambda *_: (0,) * nd)


def _load_once(src_hbm, dst_vmem, sem):
    @pl.when(pl.program_id(0) == 0)
    def _():
        copy = pltpu.make_async_copy(src_hbm, dst_vmem, sem)
        copy.start()
        copy.wait()


def _rms(x, g):
    ms = jnp.mean(x * x, axis=-1, keepdims=True)
    return x * lax.rsqrt(ms + NORM_EPS) * g


def _in_proj_kernel(xp_ref, xs_ref, g1_ref, w_hbm, qg_ref, kg_ref, cos_ref, sin_ref,
                    gb_ref, u_ref, qT_ref, k_ref, vT_ref, w_ref, w_sem, *, n_prompt_tiles, conv_w, n_q, q_scale):
    _load_once(w_hbm, w_ref, w_sem)
    i = pl.program_id(0)
    x = jnp.where(i < n_prompt_tiles, xp_ref[...], xs_ref[...])
    h = _rms(x, g1_ref[...]).astype(BF16)

    def proj(c0, c1):
        return jnp.dot(h, w_ref[:, c0:c1], preferred_element_type=F32)

    cos = cos_ref[...]
    sin = sin_ref[...]
    lane = lax.broadcasted_iota(jnp.int32, (1, HEAD_DIM), 1)
    first_half = (lane % (HEAD_DIM // 2)) < (HEAD_DIM // 4)

    def norm_rope(xh, g):
        y = _rms(xh, g)
        up = pltpu.roll(y, HEAD_DIM - HEAD_DIM // 4, 1)
        dn = pltpu.roll(y, HEAD_DIM // 4, 1)
        return y * cos + jnp.where(first_half, up, dn) * sin

    q0 = 3 * conv_w
    k0 = q0 + n_q * HEAD_DIM
    q = proj(q0, k0)
    kv = proj(k0, k0 + 2 * N_KV_HEADS * HEAD_DIM)

    def q_head(hd):
        qh = norm_rope(q[:, hd * HEAD_DIM:(hd + 1) * HEAD_DIM], qg_ref[...]) * q_scale
        qT_ref[hd] = qh.T.astype(BF16)

    def kv_head(hd):
        kh = norm_rope(kv[:, hd * HEAD_DIM:(hd + 1) * HEAD_DIM], kg_ref[...])
        k_ref[:, hd * HEAD_DIM:(hd + 1) * HEAD_DIM] = kh.astype(BF16)
        vh = kv[:, (N_KV_HEADS + hd) * HEAD_DIM:(N_KV_HEADS + hd + 1) * HEAD_DIM]
        vT_ref[hd, 0] = vh.T.astype(BF16)

    head_work = ([functools.partial(q_head, hd) for hd in range(n_q)]
                 + [functools.partial(kv_head, hd) for hd in range(N_KV_HEADS)])
    cw = HEAD_DIM * 2
    n_cc = conv_w // cw
    for c in range(n_cc):
        cols = slice(c * cw, (c + 1) * cw)
        gb_ref[:, cols] = proj(c * cw, (c + 1) * cw)
        if 2 * c < len(head_work):
            head_work[2 * c]()
        u_ref[:, cols] = proj(conv_w + c * cw, conv_w + (c + 1) * cw) * proj(2 * conv_w + c * cw,
                                                                            2 * conv_w + (c + 1) * cw)
        if 2 * c + 1 < len(head_work):
            head_work[2 * c + 1]()
    for work in head_work[2 * n_cc:]:
        work()


def _in_proj(xp, xs, g1, w_in, qg, kg, cos, sin_signed, *, s_p, s_s, kchunk):
    n_p, d = xp.shape
    n_s = xs.shape[0]
    n = n_p + n_s
    conv_w = d // 2
    n_q = (d - conv_w) // HEAD_DIM
    tm = _pick(256, math.gcd(s_p, s_s))
    npt, spt, sst = n_p // tm, s_p // tm, s_s // tm
    r = kchunk // tm
    q_scale = LOG2E / math.sqrt(HEAD_DIM)

    def pos_map(i):
        return (jnp.where(i < npt, i % spt, (i - npt) % sst), 0)

    kernel = functools.partial(_in_proj_kernel, n_prompt_tiles=npt, conv_w=conv_w, n_q=n_q, q_scale=q_scale)
    return pl.pallas_call(
        kernel,
        grid=(n // tm,),
        in_specs=[
            pl.BlockSpec((tm, d), lambda i: (jnp.minimum(i, npt - 1), 0)),
            pl.BlockSpec((tm, d), lambda i: (jnp.maximum(i - npt, 0), 0)),
            _resident((1, d)),
            pl.BlockSpec(memory_space=pl.ANY),
            _resident((1, HEAD_DIM)),
            _resident((1, HEAD_DIM)),
            pl.BlockSpec((tm, HEAD_DIM), pos_map),
            pl.BlockSpec((tm, HEAD_DIM), pos_map),
        ],
        out_specs=[
            pl.BlockSpec((tm, conv_w), lambda i: (i, 0)),
            pl.BlockSpec((tm, conv_w), lambda i: (i, 0)),
            pl.BlockSpec((n_q, HEAD_DIM, tm), lambda i: (0, 0, i)),
            pl.BlockSpec((tm, N_KV_HEADS * HEAD_DIM), lambda i: (i, 0)),
            pl.BlockSpec((N_KV_HEADS, 1, HEAD_DIM, tm), lambda i: (0, i // r, 0, i % r)),
        ],
        out_shape=[
            jax.ShapeDtypeStruct((n, conv_w), F32),
            jax.ShapeDtypeStruct((n, conv_w), F32),
            jax.ShapeDtypeStruct((n_q, HEAD_DIM, n), BF16),
            jax.ShapeDtypeStruct((n, N_KV_HEADS * HEAD_DIM), BF16),
            jax.ShapeDtypeStruct((N_KV_HEADS, n // kchunk, HEAD_DIM, kchunk), BF16),
        ],
        scratch_shapes=[pltpu.VMEM(w_in.shape, w_in.dtype), pltpu.SemaphoreType.DMA(())],
        compiler_params=_params(1),
        name="in_proj",
    )(xp, xs, g1, w_in, qg, kg, cos, sin_signed)


def _attention_kernel(qT_ref, k_ref, vT_ref, w_ref, o_ref, wb_ref, m_ref, l_ref, acc_ref, s_ref, mx_ref,
                      *, n_chunks, kchunk, ksub, group):
    wb_ref[...] = w_ref[...].astype(BF16)

    m_ref[...] = jnp.full(m_ref.shape, -jnp.inf, F32)
    l_ref[...] = jnp.zeros(l_ref.shape, F32)
    acc_ref[...] = jnp.zeros(acc_ref.shape, F32)
    sub = V7X_SUBLANES

    n_sub = kchunk // ksub

    def scores_sub(c, g, par, j, mx):
        r0 = pl.multiple_of(c * kchunk + j * ksub, ksub)
        sT = jnp.dot(k_ref[pl.ds(r0, ksub), :], qT_ref[g], preferred_element_type=F32)
        s_ref[par, j * ksub:(j + 1) * ksub, :] = sT
        cm = jnp.max(sT.reshape(ksub // sub, sub, sT.shape[1]), axis=0)
        return cm if mx is None else jnp.maximum(mx, cm)

    def consume_sub(c, par, j, m_new, pv, l):
        p = jnp.exp2(s_ref[par, j * ksub:(j + 1) * ksub, :] - m_new)
        d = jnp.dot(vT_ref[0, c, :, j * ksub:(j + 1) * ksub], p.astype(BF16), preferred_element_type=F32)
        ps = jnp.sum(p.reshape(ksub // sub, sub, p.shape[1]), axis=0)
        return (d if pv is None else pv + d), (ps if l is None else l + ps)

    def pair(c_next, g_next, par_next, c, g, par):
        m_old = m_ref[g]
        m_new = jnp.maximum(m_old, jnp.max(mx_ref[par], axis=0, keepdims=True))
        alpha = jnp.exp2(m_old - m_new)
        mx = pv = l = None
        for j in range(n_sub):
            mx = scores_sub(c_next, g_next, par_next, j, mx)
            pv, l = consume_sub(c, par, j, m_new, pv, l)
        mx_ref[par_next] = mx
        l_ref[g] = alpha * l_ref[g] + jnp.sum(l, axis=0, keepdims=True)
        acc_ref[g] = alpha * acc_ref[g] + pv
        m_ref[g] = m_new

    def parity(c, g):
        return (g % 2) if group % 2 == 0 else (c * group + g) % 2

    mx0 = None
    for j in range(n_sub):
        mx0 = scores_sub(0, 0, 0, j, mx0)
    mx_ref[0] = mx0

    def chunk(c, carry):
        for g in range(group):
            if g + 1 < group:
                pair(c, g + 1, parity(c, g + 1), c, g, parity(c, g))
            else:
                pair(jnp.minimum(c + 1, n_chunks - 1), 0, parity(c + 1, 0), c, g, parity(c, g))
        return carry

    unroll = math.gcd(n_chunks, 8)
    while unroll > 1 and n_chunks // unroll < 2:
        unroll //= 2
    lax.fori_loop(0, n_chunks, chunk, 0, unroll=unroll)
    for g in range(group):
        o = acc_ref[g] / l_ref[g]
        o_ref[:, g * HEAD_DIM:(g + 1) * HEAD_DIM] = o.T.astype(o_ref.dtype)


def _attention(qT, k, vT, w, *, row_start, batch, seq, kchunk):
    n_q = qT.shape[0]
    group = n_q // N_KV_HEADS
    tq = _pick(256, seq)
    nqt = seq // tq
    q0 = row_start // tq
    s0 = row_start // seq
    n_chunks = seq // kchunk
    n_steps = batch * N_KV_HEADS * nqt
    w_rows, w_cols = w.shape
    wr = w_rows // n_steps
    assert wr * n_steps == w_rows and wr % (2 * V7X_SUBLANES) == 0, "weight rows must split evenly over the grid"
    step = lambda b, h, i: ((b * N_KV_HEADS + h) * nqt + i, 0)
    kernel = functools.partial(_attention_kernel, n_chunks=n_chunks, kchunk=kchunk, ksub=_pick(256, kchunk),
                               group=group)
    return pl.pallas_call(
        kernel,
        grid=(batch, N_KV_HEADS, nqt),
        in_specs=[
            pl.BlockSpec((group, HEAD_DIM, tq), lambda b, h, i: (h, 0, q0 + b * nqt + i)),
            pl.BlockSpec((seq, HEAD_DIM), lambda b, h, i: (s0 + b, h)),
            pl.BlockSpec((1, n_chunks, HEAD_DIM, kchunk), lambda b, h, i: (h, s0 + b, 0, 0)),
            pl.BlockSpec((wr, w_cols), step),
        ],
        out_specs=[
            pl.BlockSpec((tq, group * HEAD_DIM), lambda b, h, i: (b * nqt + i, h)),
            pl.BlockSpec((wr, w_cols), step),
        ],
        out_shape=[
            jax.ShapeDtypeStruct((batch * seq, n_q * HEAD_DIM), BF16),
            jax.ShapeDtypeStruct((w_rows, w_cols), BF16),
        ],
        scratch_shapes=[
            pltpu.VMEM((group, 1, tq), F32),
            pltpu.VMEM((group, 1, tq), F32),
            pltpu.VMEM((group, HEAD_DIM, tq), F32),
            pltpu.VMEM((2, kchunk, tq), F32),
            pltpu.VMEM((2, V7X_SUBLANES, tq), F32),
        ],
        compiler_params=_params(3),
        name="attention",
    )(qT, k, vT, w)


def _out_proj_kernel(xp_ref, xs_ref, gb_ref, u_ref, up_ref, un_ref, ap_ref, as_ref, cw_ref, cb_ref, wo_hbm,
                     g2_ref, rw_ref, rb_ref,
                     x1_ref, h2_ref, mi_ref, mw_ref, cnt_ref, ubuf, wo_ref, wo_sem, hprev,
                     *, n_prompt_tiles, n_tiles, spt, sst, conv_w, tm):
    _load_once(wo_hbm, wo_ref, wo_sem)
    step = pl.program_id(0)

    @pl.when(step == 0)
    def _():
        cnt_ref[...] = jnp.zeros(cnt_ref.shape, F32)
        hprev[...] = jnp.zeros(hprev.shape, hprev.dtype)

    routed = hprev[...]
    i = jnp.minimum(step, n_tiles - 1)
    is_p = i < n_prompt_tiles
    local = jnp.where(is_p, i % spt, (i - n_prompt_tiles) % sst)
    per_seq = jnp.where(is_p, spt, sst)
    keep_prev = (local > 0).astype(F32)
    keep_next = (local < per_seq - 1).astype(F32)

    sub = V7X_SUBLANES
    ubuf[0:sub, :] = up_ref[...] * keep_prev
    ubuf[sub:sub + tm, :] = u_ref[...]
    ubuf[sub + tm:2 * sub + tm, :] = un_ref[...] * keep_next
    conv = (ubuf[sub - 1:sub - 1 + tm, :] * cw_ref[0:1, :] + ubuf[sub:sub + tm, :] * cw_ref[1:2, :]
            + ubuf[sub + 1:sub + 1 + tm, :] * cw_ref[2:3, :] + cb_ref[...])
    conv_out = (gb_ref[...] * conv).astype(BF16)
    attn = jnp.where(is_p, ap_ref[...], as_ref[...])
    logits = jnp.dot(routed, rw_ref[...], preferred_element_type=F32) + rb_ref[...]
    lane = lax.broadcasted_iota(jnp.int32, logits.shape, 1)
    st = dict(work=logits, vals=[], hots=[])

    def select_round():
        work = st["work"]
        m = jnp.max(work, axis=-1, keepdims=True)
        idx = jnp.min(jnp.where(work == m, lane, V7X_LANES), axis=-1, keepdims=True)
        hot = lane == idx
        st["vals"].append(m)
        st["hots"].append((hot, idx))
        st["work"] = jnp.where(hot, -jnp.inf, work)

    def count_before():
        sel = jnp.zeros(logits.shape, F32)
        for hot, _ in st["hots"]:
            sel = sel + hot.astype(F32)
        sel = sel * (step > 0).astype(F32)
        row = lax.broadcasted_iota(jnp.int32, (tm, tm), 0)
        col = lax.broadcasted_iota(jnp.int32, (tm, tm), 1)
        lower = jnp.where(col < row, 1.0, 0.0).astype(BF16)
        st["before"] = jnp.dot(lower, sel.astype(BF16), preferred_element_type=F32) + cnt_ref[0:1, :]
        cnt_ref[...] = cnt_ref[...] + jnp.sum(sel, axis=0, keepdims=True)

    def write_meta():
        vals = st["vals"]
        es = [jnp.exp(v - vals[0]) for v in vals]
        den = es[0] + es[1] + es[2] + es[3]
        mi = jnp.zeros(logits.shape, jnp.int32)
        mw = jnp.zeros(logits.shape, F32)
        for kk, (hot, idx) in enumerate(st["hots"]):
            rank = jnp.sum(jnp.where(hot, st["before"], 0.0), axis=-1, keepdims=True).astype(jnp.int32)
            mi = jnp.where(lane == kk, idx, mi)
            mi = jnp.where(lane == TOP_K + kk, rank, mi)
            mw = jnp.where(lane == kk, es[kk] / den, mw)
        mi_ref[...] = mi
        mw_ref[...] = mw

    router_stages = [select_round] * TOP_K + [count_before, write_meta]
    d_out = wo_ref.shape[1]
    cw = d_out // OUT_COL_CHUNKS
    ssq = None
    for c in range(OUT_COL_CHUNKS):
        cols = slice(c * cw, (c + 1) * cw)
        x_c = jnp.where(is_p, xp_ref[:, cols], xs_ref[:, cols])
        x1_c = (x_c + jnp.dot(conv_out, wo_ref[0:conv_w, cols], preferred_element_type=F32)
                + jnp.dot(attn, wo_ref[conv_w:, cols], preferred_element_type=F32))
        x1_ref[:, cols] = x1_c
        part = jnp.sum(x1_c * x1_c, axis=-1, keepdims=True)
        ssq = part if ssq is None else ssq + part
        if c < len(router_stages):
            router_stages[c]()
    for stage in router_stages[OUT_COL_CHUNKS:]:
        stage()
    h2 = x1_ref[...] * lax.rsqrt(ssq / d_out + NORM_EPS) * g2_ref[...]
    h2_ref[...] = h2
    hprev[...] = h2.astype(BF16)


def _out_proj(xp, xs, gb, u, attn_p, attn_s, conv_w_, conv_b, w_out, g2, rw, rb, *, s_p, s_s):
    n_p, d = xp.shape
    n_s = xs.shape[0]
    n = n_p + n_s
    conv_w = gb.shape[1]
    a_w = attn_p.shape[1]
    tm = _pick(256, math.gcd(s_p, s_s))
    npt, spt, sst = n_p // tm, s_p // tm, s_s // tm
    sub = V7X_SUBLANES
    hb = tm // sub
    nt = n // tm
    kernel = functools.partial(_out_proj_kernel, n_prompt_tiles=npt, n_tiles=nt, spt=spt, sst=sst, conv_w=conv_w,
                               tm=tm)
    tile = lambda s: jnp.minimum(s, nt - 1)
    row = lambda s: (tile(s), 0)
    routed_row = lambda s: (jnp.maximum(s - 1, 0), 0)
    return pl.pallas_call(
        kernel,
        grid=(nt + 1,),
        in_specs=[
            pl.BlockSpec((tm, d), lambda s: (jnp.minimum(tile(s), npt - 1), 0)),
            pl.BlockSpec((tm, d), lambda s: (jnp.maximum(tile(s) - npt, 0), 0)),
            pl.BlockSpec((tm, conv_w), row),
            pl.BlockSpec((tm, conv_w), row),
            pl.BlockSpec((sub, conv_w), lambda s: (jnp.maximum(tile(s) * hb - 1, 0), 0)),
            pl.BlockSpec((sub, conv_w), lambda s: (jnp.minimum((tile(s) + 1) * hb, n // sub - 1), 0)),
            pl.BlockSpec((tm, a_w), lambda s: (jnp.minimum(tile(s), npt - 1), 0)),
            pl.BlockSpec((tm, a_w), lambda s: (jnp.maximum(tile(s) - npt, 0), 0)),
            _resident(conv_w_.shape),
            _resident((1, conv_w)),
            pl.BlockSpec(memory_space=pl.ANY),
            _resident((1, d)),
            _resident(rw.shape),
            _resident(rb.shape),
        ],
        out_specs=[
            pl.BlockSpec((tm, d), row),
            pl.BlockSpec((tm, d), row),
            pl.BlockSpec((tm, V7X_LANES), routed_row),
            pl.BlockSpec((tm, V7X_LANES), routed_row),
            pl.BlockSpec((V7X_SUBLANES, V7X_LANES), lambda s: (0, 0)),
        ],
        out_shape=[
            jax.ShapeDtypeStruct((n, d), F32),
            jax.ShapeDtypeStruct((n, d), F32),
            jax.ShapeDtypeStruct((n, V7X_LANES), jnp.int32),
            jax.ShapeDtypeStruct((n, V7X_LANES), F32),
            jax.ShapeDtypeStruct((V7X_SUBLANES, V7X_LANES), F32),
        ],
        scratch_shapes=[pltpu.VMEM((tm + 2 * sub, conv_w), F32), pltpu.VMEM(w_out.shape, w_out.dtype),
                        pltpu.SemaphoreType.DMA(()), pltpu.VMEM((tm, d), BF16)],
        compiler_params=_params(1),
        name="out_proj",
    )(xp, xs, gb, u, u, u, attn_p, attn_s, conv_w_, conv_b, w_out, g2, rw, rb)


def _dispatch_kernel(pos_ref, h_hbm, xs_hbm, buf, sem_in, sem_out, *, tm, n_tiles):
    i = pl.program_id(0)
    slot = i % 2

    def load(tile, s):
        return pltpu.make_async_copy(h_hbm.at[pl.ds(tile * tm, tm)], buf.at[s], sem_in.at[s])

    def row_out(s, t, dst_row):
        return pltpu.make_async_copy(buf.at[s, pl.ds(t, 1)], xs_hbm.at[pl.ds(dst_row, 1)], sem_out.at[s])

    def drain(s):
        for _ in range(TOP_K):
            pltpu.make_async_copy(buf.at[s], xs_hbm.at[pl.ds(0, tm)], sem_out.at[s]).wait()

    @pl.when(i == 0)
    def _():
        load(0, 0).start()

    @pl.when(i > 0)
    def _():
        drain(1 - slot)

    @pl.when(i + 1 < n_tiles)
    def _():
        load(i + 1, 1 - slot).start()

    load(i, slot).wait()

    def issue(t, carry):
        for kk in range(TOP_K):
            row_out(slot, t, pos_ref[0, 0, t * TOP_K + kk]).start(priority=kk % 2)
        return carry

    lax.fori_loop(0, tm, issue, 0, unroll=8)

    @pl.when(i == n_tiles - 1)
    def _():
        drain(slot)


def _dispatch(pos, h2):
    n, d = h2.shape
    tm = _pick(256, n)
    n_tiles = n // tm
    pos3 = pos.reshape(n_tiles, 1, tm * TOP_K)
    kernel = functools.partial(_dispatch_kernel, tm=tm, n_tiles=n_tiles)
    return pl.pallas_call(
        kernel,
        grid=(n_tiles,),
        in_specs=[
            pl.BlockSpec((1, 1, tm * TOP_K), lambda i: (i, 0, 0), memory_space=pltpu.SMEM),
            pl.BlockSpec(memory_space=pl.ANY),
        ],
        out_specs=pl.BlockSpec(memory_space=pl.ANY),
        out_shape=jax.ShapeDtypeStruct((n * TOP_K, d), h2.dtype),
        scratch_shapes=[pltpu.VMEM((2, tm, d), h2.dtype), pltpu.SemaphoreType.DMA((2,)),
                        pltpu.SemaphoreType.DMA((2,))],
        compiler_params=_params(1),
        name="dispatch",
    )(pos3, h2)


def _experts_kernel(vt_ref, ve_ref, vfirst_ref, vlo_ref, vhi_ref,
                    x_ref, wg_ref, wu_ref, bg_ref, bu_ref, wd_ref, bd_ref, o_ref, act_ref, *, nf, tf):
    v = pl.program_id(0)
    f = pl.program_id(1)
    lo = vlo_ref[v]
    hi = vhi_ref[v]
    tm = x_ref.shape[0]
    last = f == nf - 1
    fresh = vfirst_ref[v] == 1

    def run(r0, nr):
        rs = slice(r0, r0 + nr)
        xb = x_ref[rs, :].astype(BF16)
        gate = jnp.dot(xb, wg_ref[0], preferred_element_type=F32) + bg_ref[0]
        up = jnp.dot(xb, wu_ref[0], preferred_element_type=F32) + bu_ref[0]
        gate = jnp.minimum(gate, SWIGLU_LIMIT)
        up = jnp.clip(up, -SWIGLU_LIMIT, SWIGLU_LIMIT)
        glu = gate * jax.nn.sigmoid(gate * SWIGLU_ALPHA)
        rows = r0 + lax.broadcasted_iota(jnp.int32, (nr, 1), 0)
        mine = (rows >= lo) & (rows < hi)
        act_ref[f, rs, :] = jnp.where(mine, (up + 1.0) * glu, 0.0).astype(BF16)

        def down():
            y = jnp.dot(act_ref[0, rs, :], wd_ref[0, 0:tf, :], preferred_element_type=F32)
            for j in range(1, nf):
                y = y + jnp.dot(act_ref[j, rs, :], wd_ref[0, j * tf:(j + 1) * tf, :], preferred_element_type=F32)
            return y + jnp.where(mine, bd_ref[0], 0.0)

        @pl.when(jnp.logical_and(last, fresh))
        def _():
            o_ref[rs, :] = down()

        @pl.when(jnp.logical_and(last, jnp.logical_not(fresh)))
        def _():
            o_ref[rs, :] = o_ref[rs, :] + down()

    whole = jnp.logical_and(lo == 0, hi == tm)

    @pl.when(whole)
    def _():
        run(0, tm)

    @pl.when(jnp.logical_and(hi > lo, jnp.logical_not(whole)))
    def _():
        half = tm // 2
        for r0 in (0, half):
            touched = jnp.logical_and(lo < r0 + half, hi > r0)

            @pl.when(touched)
            def _():
                run(r0, half)

            @pl.when(jnp.logical_and(jnp.logical_not(touched), jnp.logical_and(last, fresh)))
            def _():
                o_ref[r0:r0 + half, :] = jnp.zeros((half, o_ref.shape[1]), o_ref.dtype)


def _visit_plan(counts, n_rows, tm, n_visits):
    n_exp = counts.shape[0]
    ends = jnp.cumsum(counts)
    starts = ends - counts
    first_tile = starts // tm
    last_tile = jnp.maximum(ends - 1, 0) // tm
    nvis = jnp.where(counts > 0, last_tile - first_tile + 1, 0)
    vis_end = jnp.cumsum(nvis)
    vis_start = vis_end - nvis
    total = vis_end[-1]
    v = jnp.arange(n_visits, dtype=jnp.int32)
    e = jnp.sum((v[:, None] >= vis_end[None, :]).astype(jnp.int32), axis=1)
    e = jnp.minimum(e, n_exp - 1)
    valid = v < total
    onehot = (e[:, None] == jnp.arange(n_exp, dtype=jnp.int32)[None, :]).astype(jnp.int32)
    pickv = lambda a: jnp.sum(onehot * a[None, :], axis=1)
    tile = pickv(first_tile) + (v - pickv(vis_start))
    lo = jnp.maximum(pickv(starts), tile * tm) - tile * tm
    hi = jnp.minimum(pickv(ends), (tile + 1) * tm) - tile * tm
    last_e = jnp.sum((total - 1 >= vis_end).astype(jnp.int32))
    tile = jnp.where(valid, tile, n_rows // tm - 1)
    e = jnp.where(valid, e, jnp.minimum(last_e, n_exp - 1))
    lo = jnp.where(valid, lo, 0)
    hi = jnp.where(valid, hi, 0)
    first = jnp.where(valid & (lo == 0), 1, 0)
    i32 = lambda a: a.astype(jnp.int32)
    return i32(tile), i32(e), i32(first), i32(lo), i32(hi)


def _experts(plan, xs, w_gu, b_gu, w_d, b_d):
    n_rows, d = xs.shape
    n_exp, _, two_f = w_gu.shape
    d_ff = two_f // 2
    tm = _pick(512, n_rows)
    tf = _pick(1024, d_ff)
    nf = d_ff // tf
    n_visits = plan[0].shape[0]

    def fx(f, hi, lo, v):
        return jnp.where(hi[v] > lo[v], f, 0)

    grid_spec = pltpu.PrefetchScalarGridSpec(
        num_scalar_prefetch=5,
        grid=(n_visits, nf),
        in_specs=[
            pl.BlockSpec((tm, d), lambda v, f, vt, ve, vf, lo, hi: (vt[v], 0)),
            pl.BlockSpec((1, d, tf), lambda v, f, vt, ve, vf, lo, hi: (ve[v], 0, fx(f, hi, lo, v))),
            pl.BlockSpec((1, d, tf), lambda v, f, vt, ve, vf, lo, hi: (ve[v], 0, nf + fx(f, hi, lo, v))),
            pl.BlockSpec((1, 1, tf), lambda v, f, vt, ve, vf, lo, hi: (ve[v], 0, fx(f, hi, lo, v))),
            pl.BlockSpec((1, 1, tf), lambda v, f, vt, ve, vf, lo, hi: (ve[v], 0, nf + fx(f, hi, lo, v))),
            pl.BlockSpec((1, d_ff, d), lambda v, f, vt, ve, vf, lo, hi: (ve[v], 0, 0)),
            pl.BlockSpec((1, 1, d), lambda v, f, vt, ve, vf, lo, hi: (ve[v], 0, 0)),
        ],
        out_specs=pl.BlockSpec((tm, d), lambda v, f, vt, ve, vf, lo, hi: (vt[v], 0)),
        scratch_shapes=[pltpu.VMEM((nf, tm, tf), BF16)],
    )
    return pl.pallas_call(
        functools.partial(_experts_kernel, nf=nf, tf=tf),
        grid_spec=grid_spec,
        out_shape=jax.ShapeDtypeStruct((n_rows, d), F32),
        compiler_params=_params(2, vmem=V7X_VMEM_BYTES - 4 * 1024 * 1024),
        name="experts",
    )(*plan, xs, w_gu, w_gu, b_gu, b_gu, w_d, b_d)


def _combine_kernel(pos_ref, nxt_ref, y_hbm, x1_ref, w_ref, fg_ref, o_ref, buf, sem, *, tm, n_steps):
    i = pl.program_id(0)
    slot = i % 2

    def fetch(p_ref, s):
        def body(t, carry):
            for kk in range(TOP_K):
                pltpu.make_async_copy(y_hbm.at[pl.ds(p_ref[0, 0, t * TOP_K + kk], 1)],
                                      buf.at[s, kk, pl.ds(t, 1)], sem.at[s]).start(priority=kk % 2)
            return carry
        lax.fori_loop(0, tm, body, 0, unroll=8)

    @pl.when(i == 0)
    def _():
        fetch(pos_ref, 0)

    @pl.when(i + 1 < n_steps)
    def _():
        fetch(nxt_ref, 1 - slot)

    for kk in range(TOP_K):
        pltpu.make_async_copy(y_hbm.at[pl.ds(0, tm)], buf.at[slot, kk], sem.at[slot]).wait()
    acc = x1_ref[...]
    for kk in range(TOP_K):
        acc = acc + w_ref[:, kk:kk + 1] * buf[slot, kk]
    o_ref[...] = _rms(acc, fg_ref[...])


def _combine(pos, y_sorted, x1, wts, fg, *, row_start, rows):
    n, d = x1.shape
    tm = _pick(256, rows)
    n_steps = rows // tm
    t0 = row_start // tm
    pos3 = pos.reshape(n // tm, 1, tm * TOP_K)
    kernel = functools.partial(_combine_kernel, tm=tm, n_steps=n_steps)
    smem = lambda im: pl.BlockSpec((1, 1, tm * TOP_K), im, memory_space=pltpu.SMEM)
    return pl.pallas_call(
        kernel,
        grid=(n_steps,),
        in_specs=[
            smem(lambda i: (t0 + i, 0, 0)),
            smem(lambda i: (t0 + jnp.minimum(i + 1, n_steps - 1), 0, 0)),
            pl.BlockSpec(memory_space=pl.ANY),
            pl.BlockSpec((tm, d), lambda i: (t0 + i, 0)),
            pl.BlockSpec((tm, V7X_LANES), lambda i: (t0 + i, 0)),
            _resident((1, d)),
        ],
        out_specs=pl.BlockSpec((tm, d), lambda i: (i, 0)),
        out_shape=jax.ShapeDtypeStruct((rows, d), F32),
        scratch_shapes=[pltpu.VMEM((2, TOP_K, tm, d), F32), pltpu.SemaphoreType.DMA((2,))],
        compiler_params=_params(1),
        name="combine",
    )(pos3, pos3, y_sorted, x1, wts, fg)


def _rope_tables(n_tokens):
    rows = n_tokens // GRID_W
    axis_dim = HEAD_DIM // 2
    row = jnp.repeat(jnp.arange(rows, dtype=F32), GRID_W)
    col = jnp.tile(jnp.arange(GRID_W, dtype=F32), rows)
    inv_freq = ROPE_THETA ** (-jnp.arange(0, axis_dim, 2, dtype=F32) / axis_dim)
    ang_r = row[:, None] * inv_freq[None, :]
    ang_c = col[:, None] * inv_freq[None, :]
    ang = jnp.concatenate([ang_r, ang_r, ang_c, ang_c], axis=-1)
    quarter = jnp.arange(HEAD_DIM) % (HEAD_DIM // 2) < HEAD_DIM // 4
    return jnp.cos(ang), jnp.where(quarter[None, :], -1.0, 1.0).astype(F32) * jnp.sin(ang)


def kernel(x_prompt, x_sample, norm1_g, w_in, conv_w, conv_b, q_norm_g, k_norm_g, w_out, norm2_g,
           router_w, router_b, w_gate_up, b_gate_up, w_down, b_down, final_g):
    assert norm1_g.shape[0] == 1, "one layer"
    b_p, s_p, d = x_prompt.shape
    b_s, s_s, _ = x_sample.shape
    n_p, n_s = b_p * s_p, b_s * s_s
    n = n_p + n_s
    n_exp = router_w.shape[-1]
    assert n_exp <= V7X_LANES and n_p % s_s == 0
    kchunk = _pick(1024, math.gcd(s_p, s_s))
    xp = x_prompt.reshape(n_p, d)
    xs = x_sample.reshape(n_s, d)
    cos, sin_signed = _rope_tables(max(s_p, s_s))

    gb, u, qT, k, vT = _in_proj(
        xp, xs, norm1_g[0][None, :], w_in[0].astype(BF16), q_norm_g[0][None, :], k_norm_g[0][None, :],
        cos, sin_signed, s_p=s_p, s_s=s_s, kchunk=kchunk)
    wgu, wd = w_gate_up[0], w_down[0]
    attn_p, wgu_b = _attention(qT, k, vT, wgu.reshape(-1, wgu.shape[-1]), row_start=0, batch=b_p, seq=s_p,
                               kchunk=kchunk)
    attn_s, wd_b = _attention(qT, k, vT, wd.reshape(-1, wd.shape[-1]), row_start=n_p, batch=b_s, seq=s_s,
                              kchunk=kchunk)

    rw = jnp.zeros((d, V7X_LANES), BF16).at[:, :n_exp].set(router_w[0].astype(BF16))
    rb = jnp.full((1, V7X_LANES), NEG_PAD, F32).at[0, :n_exp].set(router_b[0])
    x1, h2, meta_i, meta_w, cnt = _out_proj(
        xp, xs, gb, u, attn_p, attn_s, conv_w[0], conv_b[0][None, :], w_out[0].astype(BF16),
        norm2_g[0][None, :], rw, rb, s_p=s_p, s_s=s_s)

    counts = cnt[0, :n_exp].astype(jnp.int32)
    starts = jnp.cumsum(counts) - counts
    ids = meta_i[:, :TOP_K]
    ranks = meta_i[:, TOP_K:2 * TOP_K]
    onehot = ids[:, :, None] == jnp.arange(n_exp, dtype=jnp.int32)[None, None, :]
    pos = ranks + jnp.sum(jnp.where(onehot, starts[None, None, :], 0), axis=-1)

    x_sorted = _dispatch(pos, h2)
    n_rows = n * TOP_K
    tm_e = _pick(512, n_rows)
    plan = _visit_plan(counts, n_rows, tm_e, n_rows // tm_e + n_exp - 1)
    y_sorted = _experts(plan, x_sorted, wgu_b.reshape(wgu.shape), b_gate_up[0][:, None, :],
                        wd_b.reshape(wd.shape), b_down[0][:, None, :])

    fg = final_g[None, :]
    y_p = _combine(pos, y_sorted, x1, meta_w, fg, row_start=0, rows=n_p)
    y_s = _combine(pos, y_sorted, x1, meta_w, fg, row_start=n_p, rows=n_s)
    return (y_p.reshape(b_p, s_p, d), y_s.reshape(b_s, s_s, d))
```
